```python
import jax, jax.numpy as jnp
from jax import lax
import numpy as np

D_MODEL = 2048
BATCH = 4
SEQ = 2048
DEPTH = 2
DEC_BATCH = 8
DEC_SEQ = 16
PAST_LEN = 4096

CHUNK = 64
A_WIDTH = D_MODEL // 2
A_GROUPS = 8
A_KERNEL = 31
B_WIDTH = D_MODEL // 2
B_HEADS = 8
B_HEAD_DIM = B_WIDTH // B_HEADS
B_CHUNK = 128
C_WIDTH = D_MODEL
C_KERNEL = 3
FFN_DENSE = ((8 * D_MODEL // 3 + 255) // 256) * 256
N_EXPERTS = 8
TOP_K = 2
FFN_EXPERT = 7 * D_MODEL // 2
EPS = 1e-5

kernel_name = "hybrid_conv_gmlp_shortconv_moe_stream_step"


def _rmsnorm(x, g):
    xf = x.astype(jnp.float32)
    y = xf * lax.rsqrt(jnp.mean(xf * xf, axis=-1, keepdims=True) + EPS)
    return y.astype(x.dtype) * g


def _layernorm(x, g, b):
    xf = x.astype(jnp.float32)
    mu = jnp.mean(xf, axis=-1, keepdims=True)
    xc = xf - mu
    y = xc * lax.rsqrt(jnp.mean(xc * xc, axis=-1, keepdims=True) + EPS)
    return y.astype(x.dtype) * g + b


def _causal_dwconv(x_ext, w):
    return lax.conv_general_dilated(
        x_ext, w[:, None, :], window_strides=(1,), padding="VALID",
        dimension_numbers=("NWC", "WIO", "NWC"), feature_group_count=x_ext.shape[-1])


def _extend(x, state, k):
    if state is None:
        return jnp.pad(x, ((0, 0), (k - 1, 0), (0, 0)))
    return jnp.concatenate([state, x], axis=1)


def _spatial_gate(u, v, w_s, b_s):
    bsz, t, _ = v.shape
    L = min(t, B_CHUNK)
    n = t // L
    vh = v.reshape(bsz, n, L, B_HEADS, B_HEAD_DIM)
    pos = jnp.arange(L)
    mask = (pos[:, None] // CHUNK) >= (pos[None, :] // CHUNK)
    w = jnp.where(mask[None], w_s[:, :L, :L], 0.0)
    s = jnp.einsum("hts,bnshd->bnthd", w, vh) + b_s[:, :L].T[None, None, :, :, None]
    return u * s.reshape(bsz, t, B_WIDTH)


def _swiglu(h, w_gate, w_up, w_down):
    return (jax.nn.silu(h @ w_gate) * (h @ w_up)) @ w_down


def _moe(h, router_w, w_gate, w_up, w_down):
    hf = h.reshape(-1, D_MODEL)
    logits = hf.astype(jnp.float32) @ router_w.astype(jnp.float32)
    top_val, top_idx = lax.top_k(logits, TOP_K)
    probs = jax.nn.softmax(top_val, axis=-1)
    gates = jnp.sum(jax.nn.one_hot(top_idx, N_EXPERTS, dtype=jnp.float32) * probs[..., None], axis=-2)
    gates = gates.astype(h.dtype)
    out = jnp.zeros_like(hf)
    for e in range(N_EXPERTS):
        out = out + gates[:, e:e + 1] * _swiglu(hf, w_gate[e], w_up[e], w_down[e])
    return out.reshape(h.shape)


def _layer_even(x, conv_state, g_mix, w_in, b_in, conv_w, conv_b, ln_a_g, ln_a_b,
                ln_v_g, ln_v_b, w_s, b_s, w_out, g_ffn, w_gate, w_up, w_down):
    h = _rmsnorm(x, g_mix)
    z = h @ w_in + b_in
    a_lin, a_gate, u, v = jnp.split(z, [A_WIDTH, 2 * A_WIDTH, 2 * A_WIDTH + B_WIDTH], axis=-1)
    a = a_lin * jax.nn.sigmoid(a_gate)
    a_ext = _extend(a, conv_state, A_KERNEL)
    new_conv = a_ext[:, -(A_KERNEL - 1):]
    a = jax.nn.silu(_layernorm(_causal_dwconv(a_ext, conv_w) + conv_b, ln_a_g, ln_a_b))
    v = _layernorm(v, ln_v_g, ln_v_b)
    b_out = _spatial_gate(u, v, w_s, b_s)
    x = x + jnp.concatenate([a, b_out], axis=-1) @ w_out
    x = x + _swiglu(_rmsnorm(x, g_ffn), w_gate, w_up, w_down)
    return x, new_conv, v


def _layer_odd(x, conv_state, g_mix, w_in, conv_w, w_out, g_ffn, router_w, w_gate, w_up, w_down):
    h = _rmsnorm(x, g_mix)
    b_g, c_g, hx = jnp.split(h @ w_in, 3, axis=-1)
    p_ext = _extend(c_g * hx, conv_state, C_KERNEL)
    new_conv = p_ext[:, -(C_KERNEL - 1):]
    x = x + (b_g * _causal_dwconv(p_ext, conv_w)) @ w_out
    x = x + _moe(_rmsnorm(x, g_ffn), router_w, w_gate, w_up, w_down)
    return x, new_conv


def setup_inputs(seed: int = 0) -> dict:
    key = jax.random.key(seed)
    ks = iter(jax.random.split(key, 40))
    f32 = jnp.float32

    def w(shape, fan_in):
        return jax.random.normal(next(ks), shape, f32) * (fan_in ** -0.5)

    def gain(shape):
        return 1.0 + 0.1 * jax.random.normal(next(ks), shape, f32)

    def bias(shape):
        return 0.02 * jax.random.normal(next(ks), shape, f32)

    d_in0 = 2 * A_WIDTH + 2 * B_WIDTH
    return {
        "x_prompt": jax.random.normal(next(ks), (BATCH, SEQ, D_MODEL), f32),
        "x_sample": jax.random.normal(next(ks), (DEC_BATCH, DEC_SEQ, D_MODEL), f32),
        "cache_conv_a": 0.5 * jax.random.normal(next(ks), (DEC_BATCH, A_KERNEL - 1, A_WIDTH), f32),
        "cache_conv_c": 0.5 * jax.random.normal(next(ks), (DEC_BATCH, C_KERNEL - 1, C_WIDTH), f32),
        "l0_norm_mix": gain((D_MODEL,)),
        "l0_w_in": w((D_MODEL, d_in0), D_MODEL),
        "l0_b_in": bias((d_in0,)),
        "l0_conv_w": w((A_KERNEL, A_WIDTH), A_KERNEL),
        "l0_conv_b": bias((A_WIDTH,)),
        "l0_ln_a_g": gain((A_WIDTH,)),
        "l0_ln_a_b": bias((A_WIDTH,)),
        "l0_ln_v_g": gain((B_WIDTH,)),
        "l0_ln_v_b": bias((B_WIDTH,)),
        "l0_w_s": w((B_HEADS, B_CHUNK, B_CHUNK), B_CHUNK),
        "l0_b_s": gain((B_HEADS, B_CHUNK)),
        "l0_w_out": w((A_WIDTH + B_WIDTH, D_MODEL), A_WIDTH + B_WIDTH),
        "l0_norm_ffn": gain((D_MODEL,)),
        "l0_ffn_gate": w((D_MODEL, FFN_DENSE), D_MODEL),
        "l0_ffn_up": w((D_MODEL, FFN_DENSE), D_MODEL),
        "l0_ffn_down": w((FFN_DENSE, D_MODEL), FFN_DENSE),
        "l1_norm_mix": gain((D_MODEL,)),
        "l1_w_in": w((D_MODEL, 3 * C_WIDTH), D_MODEL),
        "l1_conv_w": w((C_KERNEL, C_WIDTH), C_KERNEL),
        "l1_w_out": w((C_WIDTH, D_MODEL), C_WIDTH),
        "l1_norm_ffn": gain((D_MODEL,)),
        "l1_router": w((D_MODEL, N_EXPERTS), D_MODEL),
        "l1_moe_gate": w((N_EXPERTS, D_MODEL, FFN_EXPERT), D_MODEL),
        "l1_moe_up": w((N_EXPERTS, D_MODEL, FFN_EXPERT), D_MODEL),
        "l1_moe_down": w((N_EXPERTS, FFN_EXPERT, D_MODEL), FFN_EXPERT),
        "final_norm": gain((D_MODEL,)),
    }


def reference(x_prompt, x_sample, cache_conv_a, cache_conv_c,
              l0_norm_mix, l0_w_in, l0_b_in, l0_conv_w, l0_conv_b, l0_ln_a_g, l0_ln_a_b,
              l0_ln_v_g, l0_ln_v_b, l0_w_s, l0_b_s, l0_w_out, l0_norm_ffn,
              l0_ffn_gate, l0_ffn_up, l0_ffn_down,
              l1_norm_mix, l1_w_in, l1_conv_w, l1_w_out, l1_norm_ffn, l1_router,
              l1_moe_gate, l1_moe_up, l1_moe_down, final_norm):
    even = (l0_norm_mix, l0_w_in, l0_b_in, l0_conv_w, l0_conv_b, l0_ln_a_g, l0_ln_a_b,
            l0_ln_v_g, l0_ln_v_b, l0_w_s, l0_b_s, l0_w_out, l0_norm_ffn,
            l0_ffn_gate, l0_ffn_up, l0_ffn_down)
    odd = (l1_norm_mix, l1_w_in, l1_conv_w, l1_w_out, l1_norm_ffn, l1_router,
           l1_moe_gate, l1_moe_up, l1_moe_down)
    xp, xs = x_prompt, x_sample
    for layer in range(DEPTH):
        if layer % 2 == 0:
            xp, conv_a_prompt, _ = _layer_even(xp, None, *even)
            xs, conv_a_sample, gmlp_v_sample = _layer_even(xs, cache_conv_a, *even)
        else:
            xp, conv_c_prompt = _layer_odd(xp, None, *odd)
            xs, conv_c_sample = _layer_odd(xs, cache_conv_c, *odd)
    y_prompt = _rmsnorm(xp, final_norm)
    y_sample = _rmsnorm(xs, final_norm)
    return (y_prompt, y_sample, conv_a_prompt, conv_a_sample, gmlp_v_sample, conv_c_prompt, conv_c_sample)
```

```python
import functools

import jax
import jax.numpy as jnp
from jax import lax
from jax.experimental import pallas as pl
from jax.experimental.pallas import tpu as pltpu

EPS = 1e-5
CAUSAL_CHUNK = 64
GATE_CHUNK = 128
TOP_K = 2

V7X_SUBLANES = 8
V7X_SCOPED_VMEM_BYTES = 60000 * 1024

MIX_ROWS = 256
CONV_ROW_BLOCK = 32
FFN_ROWS = 512
FFN_COLS = 512
MOE_SUB = 256
MOE_SUBS_PER_VISIT = 9
MOE_COLS = 256
COMBINE_ROWS = 128

F32 = jnp.float32
BF16 = jnp.bfloat16


def _round_up(n, m):
    return -(-n // m) * m


def _vmem_limit(estimate_bytes):
    return int(min(V7X_SCOPED_VMEM_BYTES, estimate_bytes + (8 << 20)))


def _rmsnorm(x, g):
    return x * lax.rsqrt(jnp.mean(x * x, axis=-1, keepdims=True) + EPS) * g


def _layernorm(x, g, b):
    mu = jnp.mean(x, axis=-1, keepdims=True)
    xc = x - mu
    return xc * lax.rsqrt(jnp.mean(xc * xc, axis=-1, keepdims=True) + EPS) * g + b


def _pack_bf16_halves(x):
    n = x.shape[-1] // 2
    bits = lax.bitcast_convert_type(x.astype(BF16).astype(F32), jnp.uint32)
    return (bits[:, n:] & jnp.uint32(0xFFFF0000)) | (bits[:, :n] >> 16)


def _unpack_bf16_halves(packed):
    lo = lax.bitcast_convert_type(packed << 16, F32)
    hi = lax.bitcast_convert_type(packed & jnp.uint32(0xFFFF0000), F32)
    return jnp.concatenate([lo, hi], axis=-1).astype(BF16)


def _resident(shape):
    nd = len(shape)
    return pl.BlockSpec(shape, lambda *_: (0,) * nd, pipeline_mode=pl.Buffered(1))


def _mixer_tiling(n_seq, seq_len):
    seg_len = min(seq_len, MIX_ROWS)
    n_seg = 1 if seq_len >= MIX_ROWS else n_seq
    tiles_per_seq = seq_len // seg_len if n_seg == 1 else 1
    assert seq_len % seg_len == 0 and (seg_len * n_seg) % (2 * V7X_SUBLANES) == 0
    return seg_len, n_seg, tiles_per_seq, n_seq // n_seg


def _causal_conv_segments(src, state0_ref, ext_ref, dst_ref, state_out_ref, cw_ref, bias,
                          *, seg_len, n_seg, first_tile):
    ksize = cw_ref.shape[0]
    width = src.shape[-1]
    pad = _round_up(ksize - 1, V7X_SUBLANES)
    off = pad - (ksize - 1)
    for s in range(n_seg):
        @pl.when(first_tile)
        def _():
            ext_ref[off:pad, :] = state0_ref[s]

        ext_ref[pad:pad + seg_len, :] = src[s * seg_len:(s + 1) * seg_len]
        for r0 in range(0, seg_len, CONV_ROW_BLOCK):
            rb = min(CONV_ROW_BLOCK, seg_len - r0)
            acc = jnp.broadcast_to(bias, (rb, width))
            for k in range(ksize):
                acc = acc + cw_ref[k:k + 1, :] * ext_ref[off + k + r0:off + k + r0 + rb, :]
            dst_ref[s * seg_len + r0:s * seg_len + r0 + rb, :] = acc
        state_out_ref[s] = ext_ref[off + seg_len:pad + seg_len, :]
        if n_seg == 1:
            ext_ref[0:pad, :] = ext_ref[seg_len:seg_len + pad, :]


def _l0_mixer_kernel(x_ref, state0_ref, g_ref, win_ref, bin_ref, cw_ref, cb_ref,
                     lag_ref, lab_ref, lvg_ref, lvb_ref, ws_ref, bs_ref, wout_ref,
                     x1_ref, state_ref, v_ref, ext_ref, y_ref, ab_ref, *, seg_len, n_seg):
    c = cb_ref.shape[-1]
    heads, lc, _ = ws_ref.shape
    hd = c // heads
    rows = seg_len * n_seg

    x = x_ref[...]
    h = _rmsnorm(x, g_ref[...]).astype(BF16)
    z = jnp.dot(h, win_ref[...], preferred_element_type=F32) + bin_ref[...]
    a = z[:, :c] * jax.nn.sigmoid(z[:, c:2 * c])
    u = z[:, 2 * c:3 * c]
    v = z[:, 3 * c:]

    _causal_conv_segments(a, state0_ref, ext_ref, y_ref, state_ref, cw_ref, cb_ref[...],
                          seg_len=seg_len, n_seg=n_seg, first_tile=pl.program_id(1) == 0)
    a_act = _layernorm(y_ref[...], lag_ref[...], lab_ref[...])
    ab_ref[:, :c] = (a_act * jax.nn.sigmoid(a_act)).astype(BF16)

    vn = _layernorm(v, lvg_ref[...], lvb_ref[...])
    v_ref[...] = vn
    vb = vn.astype(BF16)
    for ci in range(rows // lc):
        r = slice(ci * lc, (ci + 1) * lc)
        for hh in range(heads):
            cs = slice(hh * hd, (hh + 1) * hd)
            s = jnp.dot(ws_ref[hh], vb[r, cs], preferred_element_type=F32) + bs_ref[:, cs]
            ab_ref[r, c + hh * hd:c + (hh + 1) * hd] = (u[r, cs] * s).astype(BF16)

    x1_ref[...] = x + jnp.dot(ab_ref[...], wout_ref[...], preferred_element_type=F32)


def _l0_mixer(x, state0, weights, *, n_seq, seq_len, keep_v):
    g, win, b_in, cw, cb, lag, lab, lvg, lvb, ws, bs, wout = weights
    m, d = x.shape
    c = cb.shape[-1]
    ksize = cw.shape[0]
    seg_len, n_seg, tiles_per_seq, groups = _mixer_tiling(n_seq, seq_len)
    rows = seg_len * n_seg
    assert rows % GATE_CHUNK == 0
    pad = _round_up(ksize - 1, V7X_SUBLANES)

    row_map = lambda i, t: (i * tiles_per_seq + t, 0)
    state_spec = pl.BlockSpec((n_seg, ksize - 1, c), lambda i, t: (i, 0, 0))
    in_specs = [pl.BlockSpec((rows, d), row_map), state_spec] + [_resident(w.shape) for w in weights]
    out_shape = (
        jax.ShapeDtypeStruct((m, d), F32),
        jax.ShapeDtypeStruct((n_seq, ksize - 1, c), F32),
        jax.ShapeDtypeStruct((m if keep_v else rows, c), F32),
    )
    out_specs = (
        pl.BlockSpec((rows, d), row_map),
        state_spec,
        pl.BlockSpec((rows, c), row_map if keep_v else (lambda i, t: (0, 0))),
    )
    est = (2 * (win.size + wout.size) + 4 * rows * d * 4 + 4 * rows * 4 * c * 4
           + 4 * rows * c * 4 + (pad + seg_len) * c * 4)
    return pl.pallas_call(
        functools.partial(_l0_mixer_kernel, seg_len=seg_len, n_seg=n_seg),
        grid=(groups, tiles_per_seq),
        in_specs=in_specs,
        out_specs=out_specs,
        out_shape=out_shape,
        scratch_shapes=[
            pltpu.VMEM((pad + seg_len, c), F32),
            pltpu.VMEM((rows, c), F32),
            pltpu.VMEM((rows, 2 * c), BF16),
        ],
        compiler_params=pltpu.CompilerParams(
            dimension_semantics=("arbitrary", "arbitrary"), vmem_limit_bytes=_vmem_limit(est)),
        name="l0_mixer",
    )(x, state0, *weights)


def _ffn_kernel(x_ref, g_ref, wg_ref, wu_ref, wd_ref, o_ref, xn_ref, acc_ref):
    f = pl.program_id(1)

    @pl.when(f == 0)
    def _():
        xn_ref[...] = _rmsnorm(x_ref[...], g_ref[...]).astype(BF16)
        acc_ref[...] = jnp.zeros_like(acc_ref)

    xn = xn_ref[...]
    gate = jnp.dot(xn, wg_ref[...], preferred_element_type=F32)
    up = jnp.dot(xn, wu_ref[...], preferred_element_type=F32)
    hid = (gate * jax.nn.sigmoid(gate) * up).astype(BF16)
    acc_ref[...] += jnp.dot(hid, wd_ref[...], preferred_element_type=F32)

    @pl.when(f == pl.num_programs(1) - 1)
    def _():
        o_ref[...] = x_ref[...] + acc_ref[...]


def _ffn(x, g, wg, wu, wd):
    m, d = x.shape
    hidden = wg.shape[1]
    rows = min(FFN_ROWS, m)
    cols = min(FFN_COLS, hidden)
    assert m % rows == 0 and hidden % cols == 0 and rows % (2 * V7X_SUBLANES) == 0
    est = 4 * rows * d * 4 + rows * d * 4 + rows * d * 2 + 2 * 3 * d * cols * 2 + 3 * rows * cols * 4
    return pl.pallas_call(
        _ffn_kernel,
        grid=(m // rows, hidden // cols),
        in_specs=[
            pl.BlockSpec((rows, d), lambda i, f: (i, 0)),
            pl.BlockSpec((1, d), lambda i, f: (0, 0)),
            pl.BlockSpec((d, cols), lambda i, f: (0, f)),
            pl.BlockSpec((d, cols), lambda i, f: (0, f)),
            pl.BlockSpec((cols, d), lambda i, f: (f, 0)),
        ],
        out_specs=pl.BlockSpec((rows, d), lambda i, f: (i, 0)),
        out_shape=jax.ShapeDtypeStruct((m, d), F32),
        scratch_shapes=[pltpu.VMEM((rows, d), BF16), pltpu.VMEM((rows, d), F32)],
        compiler_params=pltpu.CompilerParams(
            dimension_semantics=("arbitrary", "arbitrary"), vmem_limit_bytes=_vmem_limit(est)),
        name="l0_ffn",
    )(x, g, wg, wu, wd)


def _top2_gates(logits):
    n_exp = logits.shape[-1]
    lane = lax.broadcasted_iota(jnp.int32, logits.shape, 1)
    m1 = jnp.max(logits, axis=-1, keepdims=True)
    i1 = jnp.min(jnp.where(logits == m1, lane, n_exp), axis=-1, keepdims=True)
    sel1 = lane == i1
    rest = jnp.where(sel1, -jnp.inf, logits)
    m2 = jnp.max(rest, axis=-1, keepdims=True)
    i2 = jnp.min(jnp.where(rest == m2, lane, n_exp), axis=-1, keepdims=True)
    sel2 = lane == i2
    e2 = jnp.exp(m2 - m1)
    denom = 1.0 + e2
    gates = jnp.where(sel1, 1.0 / denom, 0.0) + jnp.where(sel2, e2 / denom, 0.0)
    return gates, (sel1 | sel2).astype(jnp.int32)


def _l1_mixer_kernel(x_ref, state0_ref, g_ref, win_ref, cw_ref, wout_ref, gf_ref, rt_ref,
                     x3_ref, xn_ref, gates_ref, sel_ref, state_ref, ext_ref, y_ref,
                     *, seg_len, n_seg):
    cw = cw_ref.shape[-1]
    x = x_ref[...]
    h = _rmsnorm(x, g_ref[...]).astype(BF16)
    z = jnp.dot(h, win_ref[...], preferred_element_type=F32)
    b_g = z[:, :cw]
    p = z[:, cw:2 * cw] * z[:, 2 * cw:]
    _causal_conv_segments(p, state0_ref, ext_ref, y_ref, state_ref, cw_ref,
                          jnp.zeros((1, cw), F32),
                          seg_len=seg_len, n_seg=n_seg, first_tile=pl.program_id(1) == 0)
    q = (b_g * y_ref[...]).astype(BF16)
    x3 = x + jnp.dot(q, wout_ref[...], preferred_element_type=F32)
    x3_ref[...] = x3
    xn = _rmsnorm(x3, gf_ref[...])
    xn_ref[...] = _pack_bf16_halves(xn)
    logits = jnp.dot(xn, rt_ref[...], preferred_element_type=F32,
                     precision=lax.Precision.HIGHEST)
    gates, sel = _top2_gates(logits)
    gates_ref[...] = gates
    sel_ref[...] = sel


def _l1_mixer(x, state0, weights, *, n_seq, seq_len):
    g, win, cw, wout, gf, router = weights
    m, d = x.shape
    width = cw.shape[-1]
    ksize = cw.shape[0]
    n_exp = router.shape[-1]
    seg_len, n_seg, tiles_per_seq, groups = _mixer_tiling(n_seq, seq_len)
    rows = seg_len * n_seg
    pad = _round_up(ksize - 1, V7X_SUBLANES)

    row_map = lambda i, t: (i * tiles_per_seq + t, 0)
    state_spec = pl.BlockSpec((n_seg, ksize - 1, width), lambda i, t: (i, 0, 0))
    in_specs = [pl.BlockSpec((rows, d), row_map), state_spec] + [_resident(w.shape) for w in weights]
    out_shape = (
        jax.ShapeDtypeStruct((m, d), F32),
        jax.ShapeDtypeStruct((m, d // 2), jnp.uint32),
        jax.ShapeDtypeStruct((m, n_exp), F32),
        jax.ShapeDtypeStruct((m, n_exp), jnp.int32),
        jax.ShapeDtypeStruct((n_seq, ksize - 1, width), F32),
    )
    out_specs = (
        pl.BlockSpec((rows, d), row_map), pl.BlockSpec((rows, d // 2), row_map),
        pl.BlockSpec((rows, n_exp), row_map), pl.BlockSpec((rows, n_exp), row_map),
        state_spec,
    )
    est = (2 * (win.size + wout.size) + 6 * rows * d * 4 + 3 * rows * 3 * width * 4
           + 3 * rows * width * 4)
    return pl.pallas_call(
        functools.partial(_l1_mixer_kernel, seg_len=seg_len, n_seg=n_seg),
        grid=(groups, tiles_per_seq),
        in_specs=in_specs,
        out_specs=out_specs,
        out_shape=out_shape,
        scratch_shapes=[
            pltpu.VMEM((pad + seg_len, width), F32),
            pltpu.VMEM((rows, width), F32),
        ],
        compiler_params=pltpu.CompilerParams(
            dimension_semantics=("arbitrary", "arbitrary"), vmem_limit_bytes=_vmem_limit(est)),
        name="l1_mixer",
    )(x, state0, *weights)


def _gather_rows_kernel(src_ref, xa_hbm, xb_hbm, o_hbm, sems, *, rows):
    i = pl.program_id(0)
    n = pl.num_programs(0)
    slot = i % 2
    n_a = xa_hbm.shape[0]

    def issue(j, carry):
        t = src_ref[0, 0, j]
        dst = o_hbm.at[pl.ds(i * rows + j, 1)]

        @pl.when(t < n_a)
        def _():
            pltpu.make_async_copy(xa_hbm.at[pl.ds(t, 1)], dst, sems.at[slot]).start()

        @pl.when(t >= n_a)
        def _():
            pltpu.make_async_copy(xb_hbm.at[pl.ds(t - n_a, 1)], dst, sems.at[slot]).start()

        return carry

    lax.fori_loop(0, rows, issue, 0)

    def wait_tile(tile, sem_slot):
        rows_of_tile = o_hbm.at[pl.ds(tile * rows, rows)]
        pltpu.make_async_copy(rows_of_tile, rows_of_tile, sems.at[sem_slot]).wait()

    @pl.when(i > 0)
    def _():
        wait_tile(i - 1, 1 - slot)

    @pl.when(i == n - 1)
    def _():
        wait_tile(i, slot)


def _gather_rows(xa, xb, src, *, rows):
    n_out = src.shape[0]
    assert n_out % rows == 0 and xa.shape[1:] == xb.shape[1:] and xa.dtype == xb.dtype
    tiles = n_out // rows
    return pl.pallas_call(
        functools.partial(_gather_rows_kernel, rows=rows),
        grid=(tiles,),
        in_specs=[
            pl.BlockSpec((1, 1, rows), lambda i: (i, 0, 0), memory_space=pltpu.SMEM),
            pl.BlockSpec(memory_space=pl.ANY),
            pl.BlockSpec(memory_space=pl.ANY),
        ],
        out_specs=pl.BlockSpec(memory_space=pl.ANY),
        out_shape=jax.ShapeDtypeStruct((n_out, xa.shape[1]), xa.dtype),
        scratch_shapes=[pltpu.SemaphoreType.DMA((2,))],
        compiler_params=pltpu.CompilerParams(dimension_semantics=("arbitrary",)),
        name="moe_gather",
    )(src.reshape(tiles, 1, rows), xa, xb)


def _moe_kernel(ve_ref, vrow_ref, vsub_ref, tail_ref, xs_hbm, wg_ref, wu_ref, wd_ref,
                y_hbm, xbuf, acc, wgb, wub, wdb, sem_in, sem_out):
    del ve_ref
    v = pl.program_id(0)
    f = pl.program_id(1)
    n_sub = vsub_ref[v]
    row0 = vrow_ref[v]

    def slab_copies(src_of, dst_of, sem, count):
        def start(s, carry):
            pltpu.make_async_copy(src_of(s), dst_of(s), sem).start()
            return carry

        def wait(s, carry):
            pltpu.make_async_copy(src_of(s), dst_of(s), sem).wait()
            return carry

        lax.fori_loop(0, count, start, 0)
        lax.fori_loop(0, count, wait, 0)

    def buf_rows(ref):
        return lambda s: ref.at[pl.ds(pl.multiple_of(s * MOE_SUB, MOE_SUB), MOE_SUB)]

    def hbm_rows(ref, base):
        return lambda s: ref.at[pl.ds(pl.multiple_of(base + s * MOE_SUB, MOE_SUB), MOE_SUB)]

    @pl.when(jnp.logical_and(v == 0, f == 0))
    def _():
        acc[...] = jnp.zeros_like(acc)
        slab_copies(lambda s: acc.at[pl.ds(0, MOE_SUB)], hbm_rows(y_hbm, tail_ref[0]),
                    sem_out, tail_ref[1])

    @pl.when(n_sub > 0)
    def _():
        @pl.when(f == 0)
        def _():
            acc[...] = jnp.zeros_like(acc)
            slab_copies(hbm_rows(xs_hbm, row0), buf_rows(xbuf), sem_in, n_sub)

        wgb[...] = wg_ref[...].astype(BF16)
        wub[...] = wu_ref[...].astype(BF16)
        wdb[...] = wd_ref[...].astype(BF16)

        def sub_tile(s, carry):
            r = pl.ds(pl.multiple_of(s * MOE_SUB, MOE_SUB), MOE_SUB)
            xt = _unpack_bf16_halves(xbuf[r, :])
            gate = jnp.dot(xt, wgb[...], preferred_element_type=F32)
            up = jnp.dot(xt, wub[...], preferred_element_type=F32)
            hid = (gate * jax.nn.sigmoid(gate) * up).astype(BF16)
            acc[r, :] += jnp.dot(hid, wdb[...], preferred_element_type=F32)
            return carry

        lax.fori_loop(0, n_sub, sub_tile, 0)

        @pl.when(f == pl.num_programs(1) - 1)
        def _():
            slab_copies(buf_rows(acc), hbm_rows(y_hbm, row0), sem_out, n_sub)


def _moe_experts(xs, visit_expert, visit_row, visit_subs, tail, wg, wu, wd):
    p_rows = xs.shape[0]
    _, d, hidden = wg.shape
    assert xs.shape[1] * 2 == d
    cols = min(MOE_COLS, hidden)
    assert hidden % cols == 0
    n_f = hidden // cols
    n_visits = visit_expert.shape[0]
    slab = MOE_SUBS_PER_VISIT * MOE_SUB

    def col_tile(v, f, vs):
        return jnp.where(vs[v] > 0, f, n_f - 1)

    est = slab * d * (2 + 4) + 2 * 3 * d * cols * 4 + 3 * d * cols * 2 + 4 * MOE_SUB * d * 4
    grid_spec = pltpu.PrefetchScalarGridSpec(
        num_scalar_prefetch=4,
        grid=(n_visits, n_f),
        in_specs=[
            pl.BlockSpec(memory_space=pl.ANY),
            pl.BlockSpec((None, d, cols), lambda v, f, ve, vr, vs, tl: (ve[v], 0, col_tile(v, f, vs))),
            pl.BlockSpec((None, d, cols), lambda v, f, ve, vr, vs, tl: (ve[v], 0, col_tile(v, f, vs))),
            pl.BlockSpec((None, cols, d), lambda v, f, ve, vr, vs, tl: (ve[v], col_tile(v, f, vs), 0)),
        ],
        out_specs=pl.BlockSpec(memory_space=pl.ANY),
        scratch_shapes=[
            pltpu.VMEM((slab, d // 2), jnp.uint32),
            pltpu.VMEM((slab, d), F32),
            pltpu.VMEM((d, cols), BF16),
            pltpu.VMEM((d, cols), BF16),
            pltpu.VMEM((cols, d), BF16),
            pltpu.SemaphoreType.DMA(()),
            pltpu.SemaphoreType.DMA(()),
        ],
    )
    return pl.pallas_call(
        _moe_kernel,
        grid_spec=grid_spec,
        out_shape=jax.ShapeDtypeStruct((p_rows, d), F32),
        compiler_params=pltpu.CompilerParams(
            dimension_semantics=("arbitrary", "arbitrary"), vmem_limit_bytes=_vmem_limit(est)),
        name="moe_experts",
    )(visit_expert, visit_row, visit_subs, tail, xs, wg, wu, wd)


def _combine_kernel(pos_ref, xp_ref, xs_ref, gate_ref, g_ref, y_hbm, op_ref, os_ref, ybuf, sems,
                    *, rows, prompt_tiles):
    i = pl.program_id(0)
    n = pl.num_programs(0)
    slot = i % 2

    def issue(tile_slot, pos_row):
        def body(j, carry):
            for k in range(TOP_K):
                pltpu.make_async_copy(y_hbm.at[pl.ds(pos_ref[pos_row, 0, TOP_K * j + k], 1)],
                                      ybuf.at[tile_slot, k, pl.ds(j, 1)],
                                      sems.at[tile_slot]).start()
            return carry

        lax.fori_loop(0, rows, body, 0)

    @pl.when(i == 0)
    def _():
        issue(0, 0)

    @pl.when(i + 1 < n)
    def _():
        issue(1 - slot, 1)

    pltpu.make_async_copy(ybuf.at[slot], ybuf.at[slot], sems.at[slot]).wait()
    gate = gate_ref[...]
    x = jnp.where(i < prompt_tiles, xp_ref[...], xs_ref[...])
    out = x + gate[:, 0:1] * ybuf[slot, 0] + gate[:, 1:2] * ybuf[slot, 1]
    out = _rmsnorm(out, g_ref[...])

    @pl.when(i < prompt_tiles)
    def _():
        op_ref[...] = out

    @pl.when(i >= prompt_tiles)
    def _():
        os_ref[...] = out


def _combine(x3_p, x3_s, y_sorted, pos_tok, gate_tok, g_final):
    m_p, d = x3_p.shape
    m_s = x3_s.shape[0]
    rows = COMBINE_ROWS
    assert m_p % rows == 0 and m_s == rows
    prompt_tiles = m_p // rows
    tiles = prompt_tiles + 1
    pos2 = pos_tok.reshape(tiles, 1, rows * TOP_K)
    pos_pair = jnp.stack([pos2, jnp.roll(pos2, -1, axis=0)], axis=1).reshape(tiles * 2, 1, rows * TOP_K)
    est = 2 * TOP_K * rows * d * 4 + 8 * rows * d * 4
    return pl.pallas_call(
        functools.partial(_combine_kernel, rows=rows, prompt_tiles=prompt_tiles),
        grid=(tiles,),
        in_specs=[
            pl.BlockSpec((2, 1, rows * TOP_K), lambda i: (i, 0, 0), memory_space=pltpu.SMEM),
            pl.BlockSpec((rows, d), lambda i: (jnp.minimum(i, prompt_tiles - 1), 0)),
            pl.BlockSpec((rows, d), lambda i: (0, 0)),
            pl.BlockSpec((rows, TOP_K), lambda i: (i, 0)),
            pl.BlockSpec((1, d), lambda i: (0, 0)),
            pl.BlockSpec(memory_space=pl.ANY),
        ],
        out_specs=(
            pl.BlockSpec((rows, d), lambda i: (jnp.minimum(i, prompt_tiles - 1), 0)),
            pl.BlockSpec((rows, d), lambda i: (0, 0)),
        ),
        out_shape=(jax.ShapeDtypeStruct((m_p, d), F32), jax.ShapeDtypeStruct((m_s, d), F32)),
        scratch_shapes=[pltpu.VMEM((2, TOP_K, rows, d), F32), pltpu.SemaphoreType.DMA((2,))],
        compiler_params=pltpu.CompilerParams(
            dimension_semantics=("arbitrary",), vmem_limit_bytes=_vmem_limit(est)),
        name="moe_combine",
    )(pos_pair, x3_p, x3_s, gate_tok, g_final, y_sorted)


def _routing_tables(sel, gates):
    m, n_exp = sel.shape
    p_rows = _round_up(m * TOP_K, MOE_SUB) + n_exp * MOE_SUB
    counts = jnp.sum(sel, axis=0)
    subs = (counts + MOE_SUB - 1) // MOE_SUB
    group_rows = subs * MOE_SUB
    group_start = jnp.cumsum(group_rows) - group_rows
    rank = jnp.cumsum(sel, axis=0) - sel
    pos_full = group_start[None, :] + rank
    order = jnp.argsort(1 - sel, axis=1, stable=True)[:, :TOP_K]
    pos_tok = jnp.take_along_axis(pos_full, order, axis=1).astype(jnp.int32)
    gate_tok = jnp.take_along_axis(gates, order, axis=1)
    token = jnp.broadcast_to(jnp.arange(m, dtype=jnp.int32)[:, None], (m, TOP_K))
    src = jnp.zeros((p_rows,), jnp.int32).at[pos_tok.reshape(-1)].set(token.reshape(-1))

    n_visits = n_exp + (p_rows // MOE_SUB) // MOE_SUBS_PER_VISIT
    visits_per = (subs + MOE_SUBS_PER_VISIT - 1) // MOE_SUBS_PER_VISIT
    visit_end = jnp.cumsum(visits_per)
    vid = jnp.arange(n_visits, dtype=jnp.int32)
    expert = jnp.searchsorted(visit_end, vid, side="right").astype(jnp.int32)
    valid = vid < visit_end[-1]
    last_expert = jnp.searchsorted(visit_end, visit_end[-1] - 1, side="right").astype(jnp.int32)
    expert = jnp.where(valid, expert, last_expert)
    j = vid - (visit_end - visits_per)[expert]
    visit_subs = jnp.where(valid, jnp.clip(subs[expert] - j * MOE_SUBS_PER_VISIT, 0, MOE_SUBS_PER_VISIT), 0)
    visit_row = jnp.where(valid, group_start[expert] + j * MOE_SUBS_PER_VISIT * MOE_SUB, 0)
    total = jnp.sum(group_rows)
    tail = jnp.stack([total, (p_rows - total) // MOE_SUB]).astype(jnp.int32)
    return (src, pos_tok, gate_tok, expert, visit_row.astype(jnp.int32),
            visit_subs.astype(jnp.int32), tail)


def _gate_tables(w_s, b_s, seq_len, width):
    heads = w_s.shape[0]
    length = min(seq_len, GATE_CHUNK)
    pos = jnp.arange(length)
    mask = (pos[:, None] // CAUSAL_CHUNK) >= (pos[None, :] // CAUSAL_CHUNK)
    w = jnp.where(mask[None], w_s[:, :length, :length], 0.0)
    reps = GATE_CHUNK // length
    if reps > 1:
        w = jnp.einsum("ab,hts->hatbs", jnp.eye(reps, dtype=w.dtype), w).reshape(
            heads, GATE_CHUNK, GATE_CHUNK)
    bias = jnp.tile(b_s[:, :length].T, (reps, 1))
    bias = jnp.repeat(bias, width // heads, axis=1)
    return w.astype(BF16), bias


def kernel(x_prompt, x_sample, cache_conv_a, cache_conv_c, l0_norm_mix, l0_w_in, l0_b_in, l0_conv_w, l0_conv_b, l0_ln_a_g, l0_ln_a_b, l0_ln_v_g, l0_ln_v_b, l0_w_s, l0_b_s, l0_w_out, l0_norm_ffn, l0_ffn_gate, l0_ffn_up, l0_ffn_down, l1_norm_mix, l1_w_in, l1_conv_w, l1_w_out, l1_norm_ffn, l1_router, l1_moe_gate, l1_moe_up, l1_moe_down, final_norm):
    n_p, t_p, d = x_prompt.shape
    n_s, t_s, _ = x_sample.shape
    m_p, m_s = n_p * t_p, n_s * t_s
    a_width = l0_conv_b.shape[0]
    row = lambda vec: vec.reshape(1, -1)
    xp = x_prompt.reshape(m_p, d)
    xs = x_sample.reshape(m_s, d)

    l0_head = (row(l0_norm_mix), l0_w_in.astype(BF16), row(l0_b_in), l0_conv_w, row(l0_conv_b),
               row(l0_ln_a_g), row(l0_ln_a_b), row(l0_ln_v_g), row(l0_ln_v_b))
    wout0 = l0_w_out.astype(BF16)
    zero_a = jnp.zeros((n_p,) + cache_conv_a.shape[1:], F32)
    xp, conv_a_prompt, _ = _l0_mixer(
        xp, zero_a, l0_head + _gate_tables(l0_w_s, l0_b_s, t_p, a_width) + (wout0,),
        n_seq=n_p, seq_len=t_p, keep_v=False)
    xs, conv_a_sample, v_sample = _l0_mixer(
        xs, cache_conv_a, l0_head + _gate_tables(l0_w_s, l0_b_s, t_s, a_width) + (wout0,),
        n_seq=n_s, seq_len=t_s, keep_v=True)

    ffn_w = (row(l0_norm_ffn), l0_ffn_gate.astype(BF16), l0_ffn_up.astype(BF16),
             l0_ffn_down.astype(BF16))
    xp = _ffn(xp, *ffn_w)
    xs = _ffn(xs, *ffn_w)

    l1_w = (row(l1_norm_mix), l1_w_in.astype(BF16), l1_conv_w, l1_w_out.astype(BF16),
            row(l1_norm_ffn), l1_router)
    zero_c = jnp.zeros((n_p,) + cache_conv_c.shape[1:], F32)
    xp, xn_p, gates_p, sel_p, conv_c_prompt = _l1_mixer(xp, zero_c, l1_w, n_seq=n_p, seq_len=t_p)
    xs, xn_s, gates_s, sel_s, conv_c_sample = _l1_mixer(xs, cache_conv_c, l1_w, n_seq=n_s, seq_len=t_s)

    sel = jnp.concatenate([sel_p, sel_s], axis=0)
    gates = jnp.concatenate([gates_p, gates_s], axis=0)
    src, pos_tok, gate_tok, v_expert, v_row, v_subs, tail = _routing_tables(sel, gates)
    x_sorted = _gather_rows(xn_p, xn_s, src, rows=MOE_SUB)
    y_sorted = _moe_experts(x_sorted, v_expert, v_row, v_subs, tail,
                            l1_moe_gate, l1_moe_up, l1_moe_down)
    y_p, y_s = _combine(xp, xs, y_sorted, pos_tok, gate_tok, row(final_norm))

    return (y_p.reshape(n_p, t_p, d), y_s.reshape(n_s, t_s, d), conv_a_prompt, conv_a_sample,
            v_sample.reshape(n_s, t_s, a_width), conv_c_prompt, conv_c_sample)
```

```python
import functools

import jax
import jax.numpy as jnp
from jax import lax
from jax.experimental import pallas as pl
from jax.experimental.pallas import tpu as pltpu

EPS = 1e-5
CAUSAL_CHUNK = 64
GATE_CHUNK = 128
TOP_K = 2

V7X_SUBLANES = 8
V7X_SCOPED_VMEM_BYTES = 60000 * 1024

MIX_ROWS = 256
CONV_ROW_BLOCK = 32
FFN_ROWS = 512
FFN_COLS = 512
MOE_SUB = 256
MOE_SUBS_PER_VISIT = 9
MOE_SUB_UNROLL = 3
MOE_COLS = 256
COMBINE_ROWS = 128

F32 = jnp.float32
BF16 = jnp.bfloat16


def _round_up(n, m):
    return -(-n // m) * m


def _vmem_limit(estimate_bytes):
    return int(min(V7X_SCOPED_VMEM_BYTES, estimate_bytes + (8 << 20)))


def _rmsnorm(x, g):
    return x * lax.rsqrt(jnp.mean(x * x, axis=-1, keepdims=True) + EPS) * g


def _layernorm(x, g, b):
    mu = jnp.mean(x, axis=-1, keepdims=True)
    xc = x - mu
    return xc * lax.rsqrt(jnp.mean(xc * xc, axis=-1, keepdims=True) + EPS) * g + b


def _dot_bf16x3(a, b):
    a_hi = a.astype(BF16)
    a_lo = (a - a_hi.astype(F32)).astype(BF16)
    b_hi = b.astype(BF16)
    b_lo = (b - b_hi.astype(F32)).astype(BF16)
    dot = functools.partial(jnp.dot, preferred_element_type=F32)
    return dot(a_hi, b_hi) + (dot(a_hi, b_lo) + dot(a_lo, b_hi))


def _resident(shape):
    nd = len(shape)
    return pl.BlockSpec(shape, lambda *_: (0,) * nd, pipeline_mode=pl.Buffered(1))


def _mixer_tiling(n_seq, seq_len):
    seg_len = min(seq_len, MIX_ROWS)
    n_seg = 1 if seq_len >= MIX_ROWS else n_seq
    tiles_per_seq = seq_len // seg_len if n_seg == 1 else 1
    assert seq_len % seg_len == 0 and (seg_len * n_seg) % (2 * V7X_SUBLANES) == 0
    return seg_len, n_seg, tiles_per_seq, n_seq // n_seg


def _conv_geometry(ksize, seg_len):
    pad = _round_up(ksize - 1, V7X_SUBLANES)
    off = pad - (ksize - 1)
    residues = sorted({(off + k) % V7X_SUBLANES for k in range(ksize)} - {0})
    return pad, off, residues, pad + seg_len - V7X_SUBLANES


def _causal_conv_segments(src, state0_ref, ext_ref, shift_ref, dst_ref, state_out_ref, cw_ref,
                          bias, *, seg_len, n_seg, first_tile):
    ksize = cw_ref.shape[0]
    width = src.shape[-1]
    pad, off, residues, n_shift = _conv_geometry(ksize, seg_len)
    for s in range(n_seg):
        @pl.when(first_tile)
        def _():
            ext_ref[off:pad, :] = state0_ref[s]

        ext_ref[pad:pad + seg_len, :] = src[s * seg_len:(s + 1) * seg_len]
        for i, r in enumerate(residues):
            shift_ref[i] = ext_ref[r:r + n_shift, :]
        for r0 in range(0, seg_len, CONV_ROW_BLOCK):
            rb = min(CONV_ROW_BLOCK, seg_len - r0)
            acc = jnp.broadcast_to(bias, (rb, width))
            for k in range(ksize):
                q, r = divmod(off + k, V7X_SUBLANES)
                lo = q * V7X_SUBLANES + r0
                if r == 0:
                    tap = ext_ref[lo:lo + rb, :]
                else:
                    tap = shift_ref[residues.index(r), lo:lo + rb, :]
                acc = acc + cw_ref[k:k + 1, :] * tap
            dst_ref[s * seg_len + r0:s * seg_len + r0 + rb, :] = acc
        state_out_ref[s] = ext_ref[off + seg_len:pad + seg_len, :]
        if n_seg == 1:
            ext_ref[0:pad, :] = ext_ref[seg_len:seg_len + pad, :]


def _l0_mixer_kernel(x_ref, state0_ref, g_ref, win_ref, bin_ref, cw_ref, cb_ref,
                     lag_ref, lab_ref, lvg_ref, lvb_ref, ws_ref, bs_ref, wout_ref,
                     x1_ref, state_ref, v_ref, ext_ref, shift_ref, y_ref, ab_ref,
                     *, seg_len, n_seg):
    c = cb_ref.shape[-1]
    heads, lc, _ = ws_ref.shape
    hd = c // heads
    rows = seg_len * n_seg

    x = x_ref[...]
    h = _rmsnorm(x, g_ref[...]).astype(BF16)
    z = jnp.dot(h, win_ref[...], preferred_element_type=F32) + bin_ref[...]
    a = z[:, :c] * jax.nn.sigmoid(z[:, c:2 * c])
    u = z[:, 2 * c:3 * c]
    v = z[:, 3 * c:]

    _causal_conv_segments(a, state0_ref, ext_ref, shift_ref, y_ref, state_ref, cw_ref,
                          cb_ref[...], seg_len=seg_len, n_seg=n_seg,
                          first_tile=pl.program_id(1) == 0)
    a_act = _layernorm(y_ref[...], lag_ref[...], lab_ref[...])
    ab_ref[:, :c] = (a_act * jax.nn.sigmoid(a_act)).astype(BF16)

    vn = _layernorm(v, lvg_ref[...], lvb_ref[...])
    v_ref[...] = vn
    vb = vn.astype(BF16)
    for ci in range(rows // lc):
        r = slice(ci * lc, (ci + 1) * lc)
        for hh in range(heads):
            cs = slice(hh * hd, (hh + 1) * hd)
            s = jnp.dot(ws_ref[hh], vb[r, cs], preferred_element_type=F32) + bs_ref[:, cs]
            ab_ref[r, c + hh * hd:c + (hh + 1) * hd] = (u[r, cs] * s).astype(BF16)

    x1_ref[...] = x + jnp.dot(ab_ref[...], wout_ref[...], preferred_element_type=F32)


def _l0_mixer(x, state0, weights, *, n_seq, seq_len, keep_v):
    g, win, b_in, cw, cb, lag, lab, lvg, lvb, ws, bs, wout = weights
    m, d = x.shape
    c = cb.shape[-1]
    ksize = cw.shape[0]
    seg_len, n_seg, tiles_per_seq, groups = _mixer_tiling(n_seq, seq_len)
    rows = seg_len * n_seg
    assert rows % GATE_CHUNK == 0
    pad, _, residues, n_shift = _conv_geometry(ksize, seg_len)

    row_map = lambda i, t: (i * tiles_per_seq + t, 0)
    state_spec = pl.BlockSpec((n_seg, ksize - 1, c), lambda i, t: (i, 0, 0))
    in_specs = [pl.BlockSpec((rows, d), row_map), state_spec] + [_resident(w.shape) for w in weights]
    out_shape = (
        jax.ShapeDtypeStruct((m, d), F32),
        jax.ShapeDtypeStruct((n_seq, ksize - 1, c), F32),
        jax.ShapeDtypeStruct((m if keep_v else rows, c), F32),
    )
    out_specs = (
        pl.BlockSpec((rows, d), row_map),
        state_spec,
        pl.BlockSpec((rows, c), row_map if keep_v else (lambda i, t: (0, 0))),
    )
    est = (2 * (win.size + wout.size) + 4 * rows * d * 4 + 4 * rows * 4 * c * 4
           + 4 * rows * c * 4 + (1 + len(residues)) * (pad + seg_len) * c * 4)
    return pl.pallas_call(
        functools.partial(_l0_mixer_kernel, seg_len=seg_len, n_seg=n_seg),
        grid=(groups, tiles_per_seq),
        in_specs=in_specs,
        out_specs=out_specs,
        out_shape=out_shape,
        scratch_shapes=[
            pltpu.VMEM((pad + seg_len, c), F32),
            pltpu.VMEM((len(residues), n_shift, c), F32),
            pltpu.VMEM((rows, c), F32),
            pltpu.VMEM((rows, 2 * c), BF16),
        ],
        compiler_params=pltpu.CompilerParams(
            dimension_semantics=("arbitrary", "arbitrary"), vmem_limit_bytes=_vmem_limit(est)),
        name="l0_mixer",
    )(x, state0, *weights)


def _ffn_kernel(x_ref, g_ref, wg_ref, wu_ref, wd_ref, o_ref, xn_ref, acc_ref):
    f = pl.program_id(1)

    @pl.when(f == 0)
    def _():
        xn_ref[...] = _rmsnorm(x_ref[...], g_ref[...]).astype(BF16)
        acc_ref[...] = jnp.zeros_like(acc_ref)

    xn = xn_ref[...]
    gate = jnp.dot(xn, wg_ref[...], preferred_element_type=F32)
    up = jnp.dot(xn, wu_ref[...], preferred_element_type=F32)
    hid = (gate * jax.nn.sigmoid(gate) * up).astype(BF16)
    acc_ref[...] += jnp.dot(hid, wd_ref[...], preferred_element_type=F32)

    @pl.when(f == pl.num_programs(1) - 1)
    def _():
        o_ref[...] = x_ref[...] + acc_ref[...]


def _ffn(x, g, wg, wu, wd):
    m, d = x.shape
    hidden = wg.shape[1]
    rows = min(FFN_ROWS, m)
    cols = min(FFN_COLS, hidden)
    assert m % rows == 0 and hidden % cols == 0 and rows % (2 * V7X_SUBLANES) == 0
    est = 4 * rows * d * 4 + rows * d * 4 + rows * d * 2 + 2 * 3 * d * cols * 2 + 3 * rows * cols * 4
    return pl.pallas_call(
        _ffn_kernel,
        grid=(m // rows, hidden // cols),
        in_specs=[
            pl.BlockSpec((rows, d), lambda i, f: (i, 0)),
            pl.BlockSpec((1, d), lambda i, f: (0, 0)),
            pl.BlockSpec((d, cols), lambda i, f: (0, f)),
            pl.BlockSpec((d, cols), lambda i, f: (0, f)),
            pl.BlockSpec((cols, d), lambda i, f: (f, 0)),
        ],
        out_specs=pl.BlockSpec((rows, d), lambda i, f: (i, 0)),
        out_shape=jax.ShapeDtypeStruct((m, d), F32),
        scratch_shapes=[pltpu.VMEM((rows, d), BF16), pltpu.VMEM((rows, d), F32)],
        compiler_params=pltpu.CompilerParams(
            dimension_semantics=("arbitrary", "arbitrary"), vmem_limit_bytes=_vmem_limit(est)),
        name="l0_ffn",
    )(x, g, wg, wu, wd)


def _top2_gates(logits):
    n_exp = logits.shape[-1]
    lane = lax.broadcasted_iota(jnp.int32, logits.shape, 1)
    m1 = jnp.max(logits, axis=-1, keepdims=True)
    i1 = jnp.min(jnp.where(logits == m1, lane, n_exp), axis=-1, keepdims=True)
    sel1 = lane == i1
    rest = jnp.where(sel1, -jnp.inf, logits)
    m2 = jnp.max(rest, axis=-1, keepdims=True)
    i2 = jnp.min(jnp.where(rest == m2, lane, n_exp), axis=-1, keepdims=True)
    sel2 = lane == i2
    e2 = jnp.exp(m2 - m1)
    denom = 1.0 + e2
    gates = jnp.where(sel1, 1.0 / denom, 0.0) + jnp.where(sel2, e2 / denom, 0.0)
    return gates, (sel1 | sel2).astype(jnp.int32)


def _l1_mixer_kernel(x_ref, state0_ref, g_ref, win_ref, cw_ref, wout_ref, gf_ref, rt_ref,
                     x3_ref, xn_ref, gates_ref, sel_ref, state_ref, ext_ref, shift_ref, y_ref,
                     *, seg_len, n_seg):
    cw = cw_ref.shape[-1]
    x = x_ref[...]
    h = _rmsnorm(x, g_ref[...]).astype(BF16)
    z = jnp.dot(h, win_ref[...], preferred_element_type=F32)
    b_g = z[:, :cw]
    p = z[:, cw:2 * cw] * z[:, 2 * cw:]
    _causal_conv_segments(p, state0_ref, ext_ref, shift_ref, y_ref, state_ref, cw_ref,
                          jnp.zeros((1, cw), F32),
                          seg_len=seg_len, n_seg=n_seg, first_tile=pl.program_id(1) == 0)
    q = (b_g * y_ref[...]).astype(BF16)
    x3 = x + jnp.dot(q, wout_ref[...], preferred_element_type=F32)
    x3_ref[...] = x3
    xn = _rmsnorm(x3, gf_ref[...])
    xn_ref[...] = xn
    gates, sel = _top2_gates(_dot_bf16x3(xn, rt_ref[...]))
    gates_ref[...] = gates
    sel_ref[...] = sel


def _l1_mixer(x, state0, weights, *, n_seq, seq_len):
    g, win, cw, wout, gf, router = weights
    m, d = x.shape
    width = cw.shape[-1]
    ksize = cw.shape[0]
    n_exp = router.shape[-1]
    seg_len, n_seg, tiles_per_seq, groups = _mixer_tiling(n_seq, seq_len)
    rows = seg_len * n_seg
    pad, _, residues, n_shift = _conv_geometry(ksize, seg_len)

    row_map = lambda i, t: (i * tiles_per_seq + t, 0)
    state_spec = pl.BlockSpec((n_seg, ksize - 1, width), lambda i, t: (i, 0, 0))
    in_specs = [pl.BlockSpec((rows, d), row_map), state_spec] + [_resident(w.shape) for w in weights]
    out_shape = (
        jax.ShapeDtypeStruct((m, d), F32),
        jax.ShapeDtypeStruct((m, d), F32),
        jax.ShapeDtypeStruct((m, n_exp), F32),
        jax.ShapeDtypeStruct((m, n_exp), jnp.int32),
        jax.ShapeDtypeStruct((n_seq, ksize - 1, width), F32),
    )
    out_specs = (
        pl.BlockSpec((rows, d), row_map), pl.BlockSpec((rows, d), row_map),
        pl.BlockSpec((rows, n_exp), row_map), pl.BlockSpec((rows, n_exp), row_map),
        state_spec,
    )
    est = (2 * (win.size + wout.size) + 6 * rows * d * 4 + 3 * rows * 3 * width * 4
           + 3 * rows * width * 4)
    return pl.pallas_call(
        functools.partial(_l1_mixer_kernel, seg_len=seg_len, n_seg=n_seg),
        grid=(groups, tiles_per_seq),
        in_specs=in_specs,
        out_specs=out_specs,
        out_shape=out_shape,
        scratch_shapes=[
            pltpu.VMEM((pad + seg_len, width), F32),
            pltpu.VMEM((len(residues), n_shift, width), F32),
            pltpu.VMEM((rows, width), F32),
        ],
        compiler_params=pltpu.CompilerParams(
            dimension_semantics=("arbitrary", "arbitrary"), vmem_limit_bytes=_vmem_limit(est)),
        name="l1_mixer",
    )(x, state0, *weights)


def _next_tile_table(table):
    tiles, n = table.shape
    return jnp.stack([table, jnp.roll(table, -1, axis=0)], axis=1).reshape(2 * tiles, 1, n)


def _gather_rows_kernel(src_ref, xa_hbm, xb_hbm, o_ref, buf, sems, *, rows):
    i = pl.program_id(0)
    n = pl.num_programs(0)
    slot = i % 2
    n_a = xa_hbm.shape[0]

    def issue(tile_slot, table_row):
        def body(j, carry):
            t = src_ref[table_row, 0, j]
            dst = buf.at[tile_slot, pl.ds(j, 1)]

            @pl.when(t < n_a)
            def _():
                pltpu.make_async_copy(xa_hbm.at[pl.ds(t, 1)], dst, sems.at[tile_slot]).start()

            @pl.when(t >= n_a)
            def _():
                pltpu.make_async_copy(xb_hbm.at[pl.ds(t - n_a, 1)], dst, sems.at[tile_slot]).start()

            return carry

        lax.fori_loop(0, rows, body, 0)

    @pl.when(i == 0)
    def _():
        issue(0, 0)

    @pl.when(i + 1 < n)
    def _():
        issue(1 - slot, 1)

    pltpu.make_async_copy(buf.at[slot], buf.at[slot], sems.at[slot]).wait()
    o_ref[...] = buf[slot].astype(o_ref.dtype)


def _gather_rows(xa, xb, src, *, rows, out_dtype):
    n_out = src.shape[0]
    width = xa.shape[1]
    assert n_out % rows == 0 and xa.shape[1:] == xb.shape[1:] and xa.dtype == xb.dtype
    tiles = n_out // rows
    return pl.pallas_call(
        functools.partial(_gather_rows_kernel, rows=rows),
        grid=(tiles,),
        in_specs=[
            pl.BlockSpec((2, 1, rows), lambda i: (i, 0, 0), memory_space=pltpu.SMEM),
            pl.BlockSpec(memory_space=pl.ANY),
            pl.BlockSpec(memory_space=pl.ANY),
        ],
        out_specs=pl.BlockSpec((rows, width), lambda i: (i, 0)),
        out_shape=jax.ShapeDtypeStruct((n_out, width), out_dtype),
        scratch_shapes=[pltpu.VMEM((2, rows, width), xa.dtype), pltpu.SemaphoreType.DMA((2,))],
        compiler_params=pltpu.CompilerParams(dimension_semantics=("arbitrary",)),
        name="moe_gather",
    )(_next_tile_table(src.reshape(tiles, rows)), xa, xb)


def _moe_kernel(ve_ref, vrow_ref, vsub_ref, tail_ref, xs_hbm, wg_ref, wu_ref, wd_ref,
                y_hbm, xbuf, acc, wgb, wub, wdb, sem_in, sem_out):
    del ve_ref
    v = pl.program_id(0)
    f = pl.program_id(1)
    n_sub = vsub_ref[v]
    row0 = vrow_ref[v]

    def slab_copies(src_of, dst_of, sem, count):
        def start(s, carry):
            pltpu.make_async_copy(src_of(s), dst_of(s), sem).start()
            return carry

        def wait(s, carry):
            pltpu.make_async_copy(src_of(s), dst_of(s), sem).wait()
            return carry

        lax.fori_loop(0, count, start, 0)
        lax.fori_loop(0, count, wait, 0)

    def buf_rows(ref):
        return lambda s: ref.at[pl.ds(pl.multiple_of(s * MOE_SUB, MOE_SUB), MOE_SUB)]

    def hbm_rows(ref, base):
        return lambda s: ref.at[pl.ds(pl.multiple_of(base + s * MOE_SUB, MOE_SUB), MOE_SUB)]

    @pl.when(jnp.logical_and(v == 0, f == 0))
    def _():
        acc[...] = jnp.zeros_like(acc)
        slab_copies(lambda s: acc.at[pl.ds(0, MOE_SUB)], hbm_rows(y_hbm, tail_ref[0]),
                    sem_out, tail_ref[1])

    @pl.when(n_sub > 0)
    def _():
        @pl.when(f == 0)
        def _():
            acc[...] = jnp.zeros_like(acc)
            slab_copies(hbm_rows(xs_hbm, row0), buf_rows(xbuf), sem_in, n_sub)

        wgb[...] = wg_ref[...].astype(BF16)
        wub[...] = wu_ref[...].astype(BF16)
        wdb[...] = wd_ref[...].astype(BF16)

        def sub_tile(s):
            r = pl.ds(pl.multiple_of(s * MOE_SUB, MOE_SUB), MOE_SUB)
            xt = xbuf[r, :]
            gate = jnp.dot(xt, wgb[...], preferred_element_type=F32)
            up = jnp.dot(xt, wub[...], preferred_element_type=F32)
            hid = (gate * jax.nn.sigmoid(gate) * up).astype(BF16)
            acc[r, :] += jnp.dot(hid, wdb[...], preferred_element_type=F32)

        def group(i, carry):
            for j in range(MOE_SUB_UNROLL):
                sub_tile(i * MOE_SUB_UNROLL + j)
            return carry

        def single(s, carry):
            sub_tile(s)
            return carry

        n_group = n_sub // MOE_SUB_UNROLL
        lax.fori_loop(0, n_group, group, 0)
        lax.fori_loop(n_group * MOE_SUB_UNROLL, n_sub, single, 0)

        @pl.when(f == pl.num_programs(1) - 1)
        def _():
            slab_copies(buf_rows(acc), hbm_rows(y_hbm, row0), sem_out, n_sub)


def _moe_experts(xs, visit_expert, visit_row, visit_subs, tail, wg, wu, wd):
    p_rows = xs.shape[0]
    _, d, hidden = wg.shape
    assert xs.shape[1] == d and xs.dtype == BF16
    cols = min(MOE_COLS, hidden)
    assert hidden % cols == 0
    n_f = hidden // cols
    n_visits = visit_expert.shape[0]
    slab = MOE_SUBS_PER_VISIT * MOE_SUB

    def col_tile(v, f, vs):
        return jnp.where(vs[v] > 0, f, n_f - 1)

    est = slab * d * (2 + 4) + 2 * 3 * d * cols * 4 + 3 * d * cols * 2 + 4 * MOE_SUB * d * 4
    grid_spec = pltpu.PrefetchScalarGridSpec(
        num_scalar_prefetch=4,
        grid=(n_visits, n_f),
        in_specs=[
            pl.BlockSpec(memory_space=pl.ANY),
            pl.BlockSpec((None, d, cols), lambda v, f, ve, vr, vs, tl: (ve[v], 0, col_tile(v, f, vs))),
            pl.BlockSpec((None, d, cols), lambda v, f, ve, vr, vs, tl: (ve[v], 0, col_tile(v, f, vs))),
            pl.BlockSpec((None, cols, d), lambda v, f, ve, vr, vs, tl: (ve[v], col_tile(v, f, vs), 0)),
        ],
        out_specs=pl.BlockSpec(memory_space=pl.ANY),
        scratch_shapes=[
            pltpu.VMEM((slab, d), BF16),
            pltpu.VMEM((slab, d), F32),
            pltpu.VMEM((d, cols), BF16),
            pltpu.VMEM((d, cols), BF16),
            pltpu.VMEM((cols, d), BF16),
            pltpu.SemaphoreType.DMA(()),
            pltpu.SemaphoreType.DMA(()),
        ],
    )
    return pl.pallas_call(
        _moe_kernel,
        grid_spec=grid_spec,
        out_shape=jax.ShapeDtypeStruct((p_rows, d), F32),
        compiler_params=pltpu.CompilerParams(
            dimension_semantics=("arbitrary", "arbitrary"), vmem_limit_bytes=_vmem_limit(est)),
        name="moe_experts",
    )(visit_expert, visit_row, visit_subs, tail, xs, wg, wu, wd)


def _combine_kernel(pos_ref, xp_ref, xs_ref, gate_ref, g_ref, y_hbm, op_ref, os_ref, ybuf, sems,
                    *, rows, prompt_tiles):
    i = pl.program_id(0)
    n = pl.num_programs(0)
    slot = i % 2

    def issue(tile_slot, pos_row):
        def body(j, carry):
            for k in range(TOP_K):
                pltpu.make_async_copy(y_hbm.at[pl.ds(pos_ref[pos_row, 0, TOP_K * j + k], 1)],
                                      ybuf.at[tile_slot, k, pl.ds(j, 1)],
                                      sems.at[tile_slot]).start()
            return carry

        lax.fori_loop(0, rows, body, 0)

    @pl.when(i == 0)
    def _():
        issue(0, 0)

    @pl.when(i + 1 < n)
    def _():
        issue(1 - slot, 1)

    pltpu.make_async_copy(ybuf.at[slot], ybuf.at[slot], sems.at[slot]).wait()
    gate = gate_ref[...]
    x = jnp.where(i < prompt_tiles, xp_ref[...], xs_ref[...])
    out = x + gate[:, 0:1] * ybuf[slot, 0] + gate[:, 1:2] * ybuf[slot, 1]
    out = _rmsnorm(out, g_ref[...])

    @pl.when(i < prompt_tiles)
    def _():
        op_ref[...] = out

    @pl.when(i >= prompt_tiles)
    def _():
        os_ref[...] = out


def _combine(x3_p, x3_s, y_sorted, pos_tok, gate_tok, g_final):
    m_p, d = x3_p.shape
    m_s = x3_s.shape[0]
    rows = COMBINE_ROWS
    assert m_p % rows == 0 and m_s == rows
    prompt_tiles = m_p // rows
    tiles = prompt_tiles + 1
    pos_pair = _next_tile_table(pos_tok.reshape(tiles, rows * TOP_K))
    est = 2 * TOP_K * rows * d * 4 + 8 * rows * d * 4
    return pl.pallas_call(
        functools.partial(_combine_kernel, rows=rows, prompt_tiles=prompt_tiles),
        grid=(tiles,),
        in_specs=[
            pl.BlockSpec((2, 1, rows * TOP_K), lambda i: (i, 0, 0), memory_space=pltpu.SMEM),
            pl.BlockSpec((rows, d), lambda i: (jnp.minimum(i, prompt_tiles - 1), 0)),
            pl.BlockSpec((rows, d), lambda i: (0, 0)),
            pl.BlockSpec((rows, TOP_K), lambda i: (i, 0)),
            pl.BlockSpec((1, d), lambda i: (0, 0)),
            pl.BlockSpec(memory_space=pl.ANY),
        ],
        out_specs=(
            pl.BlockSpec((rows, d), lambda i: (jnp.minimum(i, prompt_tiles - 1), 0)),
            pl.BlockSpec((rows, d), lambda i: (0, 0)),
        ),
        out_shape=(jax.ShapeDtypeStruct((m_p, d), F32), jax.ShapeDtypeStruct((m_s, d), F32)),
        scratch_shapes=[pltpu.VMEM((2, TOP_K, rows, d), F32), pltpu.SemaphoreType.DMA((2,))],
        compiler_params=pltpu.CompilerParams(
            dimension_semantics=("arbitrary",), vmem_limit_bytes=_vmem_limit(est)),
        name="moe_combine",
    )(pos_pair, x3_p, x3_s, gate_tok, g_final, y_sorted)


def _routing_tables(sel, gates):
    m, n_exp = sel.shape
    p_rows = _round_up(m * TOP_K, MOE_SUB) + n_exp * MOE_SUB
    counts = jnp.sum(sel, axis=0)
    subs = (counts + MOE_SUB - 1) // MOE_SUB
    group_rows = subs * MOE_SUB
    group_start = jnp.cumsum(group_rows) - group_rows
    rank = jnp.cumsum(sel, axis=0) - sel
    pos_full = group_start[None, :] + rank
    lane = jnp.arange(n_exp, dtype=jnp.int32)[None, :]
    first = jnp.argmax(sel, axis=1).astype(jnp.int32)
    second = jnp.argmax(jnp.where(lane == first[:, None], 0, sel), axis=1).astype(jnp.int32)
    order = jnp.stack([first, second], axis=1)
    pos_tok = jnp.take_along_axis(pos_full, order, axis=1).astype(jnp.int32)
    gate_tok = jnp.take_along_axis(gates, order, axis=1)
    token = jnp.broadcast_to(jnp.arange(m, dtype=jnp.int32)[:, None], (m, TOP_K))
    src = jnp.zeros((p_rows,), jnp.int32).at[pos_tok.reshape(-1)].set(token.reshape(-1))

    n_visits = n_exp + (p_rows // MOE_SUB) // MOE_SUBS_PER_VISIT
    visits_per = (subs + MOE_SUBS_PER_VISIT - 1) // MOE_SUBS_PER_VISIT
    visit_end = jnp.cumsum(visits_per)
    vid = jnp.arange(n_visits, dtype=jnp.int32)
    valid = vid < visit_end[-1]
    owner = jnp.minimum(vid, visit_end[-1] - 1)
    expert = jnp.sum(visit_end[None, :] <= owner[:, None], axis=1).astype(jnp.int32)
    j = vid - (visit_end - visits_per)[expert]
    visit_subs = jnp.where(valid, jnp.clip(subs[expert] - j * MOE_SUBS_PER_VISIT, 0, MOE_SUBS_PER_VISIT), 0)
    visit_row = jnp.where(valid, group_start[expert] + j * MOE_SUBS_PER_VISIT * MOE_SUB, 0)
    total = jnp.sum(group_rows)
    tail = jnp.stack([total, (p_rows - total) // MOE_SUB]).astype(jnp.int32)
    return (src, pos_tok, gate_tok, expert, visit_row.astype(jnp.int32),
            visit_subs.astype(jnp.int32), tail)


def _gate_tables(w_s, b_s, seq_len, width):
    heads = w_s.shape[0]
    length = min(seq_len, GATE_CHUNK)
    pos = jnp.arange(length)
    mask = (pos[:, None] // CAUSAL_CHUNK) >= (pos[None, :] // CAUSAL_CHUNK)
    w = jnp.where(mask[None], w_s[:, :length, :length], 0.0)
    reps = GATE_CHUNK // length
    if reps > 1:
        w = jnp.einsum("ab,hts->hatbs", jnp.eye(reps, dtype=w.dtype), w).reshape(
            heads, GATE_CHUNK, GATE_CHUNK)
    bias = jnp.tile(b_s[:, :length].T, (reps, 1))
    bias = jnp.repeat(bias, width // heads, axis=1)
    return w.astype(BF16), bias


def kernel(x_prompt, x_sample, cache_conv_a, cache_conv_c, l0_norm_mix, l0_w_in, l0_b_in, l0_conv_w, l0_conv_b, l0_ln_a_g, l0_ln_a_b, l0_ln_v_g, l0_ln_v_b, l0_w_s, l0_b_s, l0_w_out, l0_norm_ffn, l0_ffn_gate, l0_ffn_up, l0_ffn_down, l1_norm_mix, l1_w_in, l1_conv_w, l1_w_out, l1_norm_ffn, l1_router, l1_moe_gate, l1_moe_up, l1_moe_down, final_norm):
    n_p, t_p, d = x_prompt.shape
    n_s, t_s, _ = x_sample.shape
    m_p, m_s = n_p * t_p, n_s * t_s
    a_width = l0_conv_b.shape[0]
    row = lambda vec: vec.reshape(1, -1)
    xp = x_prompt.reshape(m_p, d)
    xs = x_sample.reshape(m_s, d)

    l0_head = (row(l0_norm_mix), l0_w_in.astype(BF16), row(l0_b_in), l0_conv_w, row(l0_conv_b),
               row(l0_ln_a_g), row(l0_ln_a_b), row(l0_ln_v_g), row(l0_ln_v_b))
    wout0 = l0_w_out.astype(BF16)
    zero_a = jnp.zeros((n_p,) + cache_conv_a.shape[1:], F32)
    xp, conv_a_prompt, _ = _l0_mixer(
        xp, zero_a, l0_head + _gate_tables(l0_w_s, l0_b_s, t_p, a_width) + (wout0,),
        n_seq=n_p, seq_len=t_p, keep_v=False)
    xs, conv_a_sample, v_sample = _l0_mixer(
        xs, cache_conv_a, l0_head + _gate_tables(l0_w_s, l0_b_s, t_s, a_width) + (wout0,),
        n_seq=n_s, seq_len=t_s, keep_v=True)

    ffn_w = (row(l0_norm_ffn), l0_ffn_gate.astype(BF16), l0_ffn_up.astype(BF16),
             l0_ffn_down.astype(BF16))
    xp = _ffn(xp, *ffn_w)
    xs = _ffn(xs, *ffn_w)

    l1_w = (row(l1_norm_mix), l1_w_in.astype(BF16), l1_conv_w, l1_w_out.astype(BF16),
            row(l1_norm_ffn), l1_router)
    zero_c = jnp.zeros((n_p,) + cache_conv_c.shape[1:], F32)
    xp, xn_p, gates_p, sel_p, conv_c_prompt = _l1_mixer(xp, zero_c, l1_w, n_seq=n_p, seq_len=t_p)
    xs, xn_s, gates_s, sel_s, conv_c_sample = _l1_mixer(xs, cache_conv_c, l1_w, n_seq=n_s, seq_len=t_s)

    sel = jnp.concatenate([sel_p, sel_s], axis=0)
    gates = jnp.concatenate([gates_p, gates_s], axis=0)
    src, pos_tok, gate_tok, v_expert, v_row, v_subs, tail = _routing_tables(sel, gates)
    x_sorted = _gather_rows(xn_p, xn_s, src, rows=MOE_SUB, out_dtype=BF16)
    y_sorted = _moe_experts(x_sorted, v_expert, v_row, v_subs, tail,
                            l1_moe_gate, l1_moe_up, l1_moe_down)
    y_p, y_s = _combine(xp, xs, y_sorted, pos_tok, gate_tok, row(final_norm))

    return (y_p.reshape(n_p, t_p, d), y_s.reshape(n_s, t_s, d), conv_a_prompt, conv_a_sample,
            v_sample.reshape(n_s, t_s, a_width), conv_c_prompt, conv_c_sample)
```

```python
import functools

import jax
import jax.numpy as jnp
from jax import lax
from jax.experimental import pallas as pl
from jax.experimental.pallas import tpu as pltpu

EPS = 1e-5
CAUSAL_CHUNK = 64
GATE_CHUNK = 128
TOP_K = 2

V7X_SUBLANES = 8
V7X_SCOPED_VMEM_BYTES = 60000 * 1024

MIX_ROWS = 256
CONV_ROW_BLOCK = 32
FFN_ROWS = 512
FFN_COLS = 512
MOE_SUB = 256
MOE_SUBS_PER_VISIT = 9
MOE_SUB_UNROLL = 3
MOE_COLS = 256
COMBINE_ROWS = 128
ROW_DMA_UNROLL = 8

F32 = jnp.float32
BF16 = jnp.bfloat16


def _round_up(n, m):
    return -(-n // m) * m


def _vmem_limit(estimate_bytes):
    return int(min(V7X_SCOPED_VMEM_BYTES, estimate_bytes + (8 << 20)))


def _rmsnorm(x, g):
    return x * lax.rsqrt(jnp.mean(x * x, axis=-1, keepdims=True) + EPS) * g


def _layernorm(x, g, b):
    mu = jnp.mean(x, axis=-1, keepdims=True)
    xc = x - mu
    return xc * lax.rsqrt(jnp.mean(xc * xc, axis=-1, keepdims=True) + EPS) * g + b


def _dot_bf16x3(a, b):
    a_hi = a.astype(BF16)
    a_lo = (a - a_hi.astype(F32)).astype(BF16)
    b_hi = b.astype(BF16)
    b_lo = (b - b_hi.astype(F32)).astype(BF16)
    dot = functools.partial(jnp.dot, preferred_element_type=F32)
    return dot(a_hi, b_hi) + (dot(a_hi, b_lo) + dot(a_lo, b_hi))


def _resident(shape):
    nd = len(shape)
    return pl.BlockSpec(shape, lambda *_: (0,) * nd, pipeline_mode=pl.Buffered(1))


def _mixer_tiling(n_seq, seq_len):
    seg_len = min(seq_len, MIX_ROWS)
    n_seg = 1 if seq_len >= MIX_ROWS else n_seq
    tiles_per_seq = seq_len // seg_len if n_seg == 1 else 1
    assert seq_len % seg_len == 0 and (seg_len * n_seg) % (2 * V7X_SUBLANES) == 0
    return seg_len, n_seg, tiles_per_seq, n_seq // n_seg


def _conv_geometry(ksize, seg_len):
    pad = _round_up(ksize - 1, V7X_SUBLANES)
    off = pad - (ksize - 1)
    residues = sorted({(off + k) % V7X_SUBLANES for k in range(ksize)} - {0})
    return pad, off, residues, pad + seg_len - V7X_SUBLANES


def _causal_conv_segments(src, state0_ref, ext_ref, shift_ref, dst_ref, state_out_ref, cw_ref,
                          bias, *, seg_len, n_seg, first_tile):
    ksize = cw_ref.shape[0]
    width = src.shape[-1]
    pad, off, residues, n_shift = _conv_geometry(ksize, seg_len)
    for s in range(n_seg):
        @pl.when(first_tile)
        def _():
            ext_ref[off:pad, :] = state0_ref[s]

        ext_ref[pad:pad + seg_len, :] = src[s * seg_len:(s + 1) * seg_len]
        for i, r in enumerate(residues):
            shift_ref[i] = ext_ref[r:r + n_shift, :]
        for r0 in range(0, seg_len, CONV_ROW_BLOCK):
            rb = min(CONV_ROW_BLOCK, seg_len - r0)
            acc = jnp.broadcast_to(bias, (rb, width))
            for k in range(ksize):
                q, r = divmod(off + k, V7X_SUBLANES)
                lo = q * V7X_SUBLANES + r0
                if r == 0:
                    tap = ext_ref[lo:lo + rb, :]
                else:
                    tap = shift_ref[residues.index(r), lo:lo + rb, :]
                acc = acc + cw_ref[k:k + 1, :] * tap
            dst_ref[s * seg_len + r0:s * seg_len + r0 + rb, :] = acc
        state_out_ref[s] = ext_ref[off + seg_len:pad + seg_len, :]
        if n_seg == 1:
            ext_ref[0:pad, :] = ext_ref[seg_len:seg_len + pad, :]


def _l0_mixer_kernel(x_ref, state0_ref, g_ref, win_ref, bin_ref, cw_ref, cb_ref,
                     lag_ref, lab_ref, lvg_ref, lvb_ref, ws_ref, bs_ref, wout_ref,
                     x1_ref, state_ref, v_ref, ext_ref, shift_ref, y_ref, ab_ref,
                     *, seg_len, n_seg):
    c = cb_ref.shape[-1]
    heads, lc, _ = ws_ref.shape
    hd = c // heads
    rows = seg_len * n_seg

    x = x_ref[...]
    h = _rmsnorm(x, g_ref[...]).astype(BF16)
    z = jnp.dot(h, win_ref[...], preferred_element_type=F32) + bin_ref[...]
    a = z[:, :c] * jax.nn.sigmoid(z[:, c:2 * c])
    u = z[:, 2 * c:3 * c]
    v = z[:, 3 * c:]

    _causal_conv_segments(a, state0_ref, ext_ref, shift_ref, y_ref, state_ref, cw_ref,
                          cb_ref[...], seg_len=seg_len, n_seg=n_seg,
                          first_tile=pl.program_id(1) == 0)
    a_act = _layernorm(y_ref[...], lag_ref[...], lab_ref[...])
    ab_ref[:, :c] = (a_act * jax.nn.sigmoid(a_act)).astype(BF16)

    vn = _layernorm(v, lvg_ref[...], lvb_ref[...])
    v_ref[...] = vn
    vb = vn.astype(BF16)
    for ci in range(rows // lc):
        r = slice(ci * lc, (ci + 1) * lc)
        for hh in range(heads):
            cs = slice(hh * hd, (hh + 1) * hd)
            s = jnp.dot(ws_ref[hh], vb[r, cs], preferred_element_type=F32) + bs_ref[:, cs]
            ab_ref[r, c + hh * hd:c + (hh + 1) * hd] = (u[r, cs] * s).astype(BF16)

    x1_ref[...] = x + jnp.dot(ab_ref[...], wout_ref[...], preferred_element_type=F32)


def _l0_mixer(x, state0, weights, *, n_seq, seq_len, keep_v):
    g, win, b_in, cw, cb, lag, lab, lvg, lvb, ws, bs, wout = weights
    m, d = x.shape
    c = cb.shape[-1]
    ksize = cw.shape[0]
    seg_len, n_seg, tiles_per_seq, groups = _mixer_tiling(n_seq, seq_len)
    rows = seg_len * n_seg
    assert rows % GATE_CHUNK == 0
    pad, _, residues, n_shift = _conv_geometry(ksize, seg_len)

    row_map = lambda i, t: (i * tiles_per_seq + t, 0)
    state_spec = pl.BlockSpec((n_seg, ksize - 1, c), lambda i, t: (i, 0, 0))
    in_specs = [pl.BlockSpec((rows, d), row_map), state_spec] + [_resident(w.shape) for w in weights]
    out_shape = (
        jax.ShapeDtypeStruct((m, d), F32),
        jax.ShapeDtypeStruct((n_seq, ksize - 1, c), F32),
        jax.ShapeDtypeStruct((m if keep_v else rows, c), F32),
    )
    out_specs = (
        pl.BlockSpec((rows, d), row_map),
        state_spec,
        pl.BlockSpec((rows, c), row_map if keep_v else (lambda i, t: (0, 0))),
    )
    est = (2 * (win.size + wout.size) + 4 * rows * d * 4 + 4 * rows * 4 * c * 4
           + 4 * rows * c * 4 + (1 + len(residues)) * (pad + seg_len) * c * 4)
    return pl.pallas_call(
        functools.partial(_l0_mixer_kernel, seg_len=seg_len, n_seg=n_seg),
        grid=(groups, tiles_per_seq),
        in_specs=in_specs,
        out_specs=out_specs,
        out_shape=out_shape,
        scratch_shapes=[
            pltpu.VMEM((pad + seg_len, c), F32),
            pltpu.VMEM((len(residues), n_shift, c), F32),
            pltpu.VMEM((rows, c), F32),
            pltpu.VMEM((rows, 2 * c), BF16),
        ],
        compiler_params=pltpu.CompilerParams(
            dimension_semantics=("arbitrary", "arbitrary"), vmem_limit_bytes=_vmem_limit(est)),
        name="l0_mixer",
    )(x, state0, *weights)


def _ffn_kernel(x_ref, g_ref, wg_ref, wu_ref, wd_ref, o_ref, xn_ref, acc_ref):
    f = pl.program_id(1)

    @pl.when(f == 0)
    def _():
        xn_ref[...] = _rmsnorm(x_ref[...], g_ref[...]).astype(BF16)
        acc_ref[...] = jnp.zeros_like(acc_ref)

    xn = xn_ref[...]
    gate = jnp.dot(xn, wg_ref[...], preferred_element_type=F32)
    up = jnp.dot(xn, wu_ref[...], preferred_element_type=F32)
    hid = (gate * jax.nn.sigmoid(gate) * up).astype(BF16)
    acc_ref[...] += jnp.dot(hid, wd_ref[...], preferred_element_type=F32)

    @pl.when(f == pl.num_programs(1) - 1)
    def _():
        o_ref[...] = x_ref[...] + acc_ref[...]


def _ffn(x, g, wg, wu, wd):
    m, d = x.shape
    hidden = wg.shape[1]
    rows = min(FFN_ROWS, m)
    cols = min(FFN_COLS, hidden)
    assert m % rows == 0 and hidden % cols == 0 and rows % (2 * V7X_SUBLANES) == 0
    est = 4 * rows * d * 4 + rows * d * 4 + rows * d * 2 + 2 * 3 * d * cols * 2 + 3 * rows * cols * 4
    return pl.pallas_call(
        _ffn_kernel,
        grid=(m // rows, hidden // cols),
        in_specs=[
            pl.BlockSpec((rows, d), lambda i, f: (i, 0)),
            pl.BlockSpec((1, d), lambda i, f: (0, 0)),
            pl.BlockSpec((d, cols), lambda i, f: (0, f)),
            pl.BlockSpec((d, cols), lambda i, f: (0, f)),
            pl.BlockSpec((cols, d), lambda i, f: (f, 0)),
        ],
        out_specs=pl.BlockSpec((rows, d), lambda i, f: (i, 0)),
        out_shape=jax.ShapeDtypeStruct((m, d), F32),
        scratch_shapes=[pltpu.VMEM((rows, d), BF16), pltpu.VMEM((rows, d), F32)],
        compiler_params=pltpu.CompilerParams(
            dimension_semantics=("arbitrary", "arbitrary"), vmem_limit_bytes=_vmem_limit(est)),
        name="l0_ffn",
    )(x, g, wg, wu, wd)


def _top2_gates(logits):
    n_exp = logits.shape[-1]
    lane = lax.broadcasted_iota(jnp.int32, logits.shape, 1)
    m1 = jnp.max(logits, axis=-1, keepdims=True)
    i1 = jnp.min(jnp.where(logits == m1, lane, n_exp), axis=-1, keepdims=True)
    sel1 = lane == i1
    rest = jnp.where(sel1, -jnp.inf, logits)
    m2 = jnp.max(rest, axis=-1, keepdims=True)
    i2 = jnp.min(jnp.where(rest == m2, lane, n_exp), axis=-1, keepdims=True)
    sel2 = lane == i2
    e2 = jnp.exp(m2 - m1)
    denom = 1.0 + e2
    gates = jnp.where(sel1, 1.0 / denom, 0.0) + jnp.where(sel2, e2 / denom, 0.0)
    return gates, (sel1 | sel2).astype(jnp.int32)


def _l1_mixer_kernel(x_ref, state0_ref, g_ref, win_ref, cw_ref, wout_ref, gf_ref, rt_ref,
                     x3_ref, xn_ref, gates_ref, sel_ref, state_ref, ext_ref, shift_ref, y_ref,
                     *, seg_len, n_seg):
    cw = cw_ref.shape[-1]
    x = x_ref[...]
    h = _rmsnorm(x, g_ref[...]).astype(BF16)
    z = jnp.dot(h, win_ref[...], preferred_element_type=F32)
    b_g = z[:, :cw]
    p = z[:, cw:2 * cw] * z[:, 2 * cw:]
    _causal_conv_segments(p, state0_ref, ext_ref, shift_ref, y_ref, state_ref, cw_ref,
                          jnp.zeros((1, cw), F32),
                          seg_len=seg_len, n_seg=n_seg, first_tile=pl.program_id(1) == 0)
    q = (b_g * y_ref[...]).astype(BF16)
    x3 = x + jnp.dot(q, wout_ref[...], preferred_element_type=F32)
    x3_ref[...] = x3
    xn = _rmsnorm(x3, gf_ref[...])
    xn_ref[...] = xn
    gates, sel = _top2_gates(_dot_bf16x3(xn, rt_ref[...]))
    gates_ref[...] = gates
    sel_ref[...] = sel


def _l1_mixer(x, state0, weights, *, n_seq, seq_len):
    g, win, cw, wout, gf, router = weights
    m, d = x.shape
    width = cw.shape[-1]
    ksize = cw.shape[0]
    n_exp = router.shape[-1]
    seg_len, n_seg, tiles_per_seq, groups = _mixer_tiling(n_seq, seq_len)
    rows = seg_len * n_seg
    pad, _, residues, n_shift = _conv_geometry(ksize, seg_len)

    row_map = lambda i, t: (i * tiles_per_seq + t, 0)
    state_spec = pl.BlockSpec((n_seg, ksize - 1, width), lambda i, t: (i, 0, 0))
    in_specs = [pl.BlockSpec((rows, d), row_map), state_spec] + [_resident(w.shape) for w in weights]
    out_shape = (
        jax.ShapeDtypeStruct((m, d), F32),
        jax.ShapeDtypeStruct((m, d), F32),
        jax.ShapeDtypeStruct((m, n_exp), F32),
        jax.ShapeDtypeStruct((m, n_exp), jnp.int32),
        jax.ShapeDtypeStruct((n_seq, ksize - 1, width), F32),
    )
    out_specs = (
        pl.BlockSpec((rows, d), row_map), pl.BlockSpec((rows, d), row_map),
        pl.BlockSpec((rows, n_exp), row_map), pl.BlockSpec((rows, n_exp), row_map),
        state_spec,
    )
    est = (2 * (win.size + wout.size) + 6 * rows * d * 4 + 3 * rows * 3 * width * 4
           + 3 * rows * width * 4)
    return pl.pallas_call(
        functools.partial(_l1_mixer_kernel, seg_len=seg_len, n_seg=n_seg),
        grid=(groups, tiles_per_seq),
        in_specs=in_specs,
        out_specs=out_specs,
        out_shape=out_shape,
        scratch_shapes=[
            pltpu.VMEM((pad + seg_len, width), F32),
            pltpu.VMEM((len(residues), n_shift, width), F32),
            pltpu.VMEM((rows, width), F32),
        ],
        compiler_params=pltpu.CompilerParams(
            dimension_semantics=("arbitrary", "arbitrary"), vmem_limit_bytes=_vmem_limit(est)),
        name="l1_mixer",
    )(x, state0, *weights)


def _next_tile_table(table):
    tiles, n = table.shape
    return jnp.stack([table, jnp.roll(table, -1, axis=0)], axis=1).reshape(2 * tiles, 1, n)


def _gather_rows_kernel(src_ref, xa_hbm, xb_hbm, o_ref, buf, sems, *, rows):
    i = pl.program_id(0)
    n = pl.num_programs(0)
    slot = i % 2
    n_a = xa_hbm.shape[0]

    def issue(tile_slot, table_row):
        def body(j, carry):
            t = src_ref[table_row, 0, j]
            dst = buf.at[tile_slot, pl.ds(j, 1)]

            @pl.when(t < n_a)
            def _():
                pltpu.make_async_copy(xa_hbm.at[pl.ds(t, 1)], dst, sems.at[tile_slot]).start()

            @pl.when(t >= n_a)
            def _():
                pltpu.make_async_copy(xb_hbm.at[pl.ds(t - n_a, 1)], dst, sems.at[tile_slot]).start()

            return carry

        lax.fori_loop(0, rows, body, 0, unroll=ROW_DMA_UNROLL)

    @pl.when(i == 0)
    def _():
        issue(0, 0)

    @pl.when(i + 1 < n)
    def _():
        issue(1 - slot, 1)

    pltpu.make_async_copy(buf.at[slot], buf.at[slot], sems.at[slot]).wait()
    o_ref[...] = buf[slot].astype(o_ref.dtype)


def _gather_rows(xa, xb, src, *, rows, out_dtype):
    n_out = src.shape[0]
    width = xa.shape[1]
    assert n_out % rows == 0 and xa.shape[1:] == xb.shape[1:] and xa.dtype == xb.dtype
    tiles = n_out // rows
    return pl.pallas_call(
        functools.partial(_gather_rows_kernel, rows=rows),
        grid=(tiles,),
        in_specs=[
            pl.BlockSpec((2, 1, rows), lambda i: (i, 0, 0), memory_space=pltpu.SMEM),
            pl.BlockSpec(memory_space=pl.ANY),
            pl.BlockSpec(memory_space=pl.ANY),
        ],
        out_specs=pl.BlockSpec((rows, width), lambda i: (i, 0)),
        out_shape=jax.ShapeDtypeStruct((n_out, width), out_dtype),
        scratch_shapes=[pltpu.VMEM((2, rows, width), xa.dtype), pltpu.SemaphoreType.DMA((2,))],
        compiler_params=pltpu.CompilerParams(dimension_semantics=("arbitrary",)),
        name="moe_gather",
    )(_next_tile_table(src.reshape(tiles, rows)), xa, xb)


def _moe_kernel(ve_ref, vrow_ref, vsub_ref, tail_ref, xs_hbm, wg_ref, wu_ref, wd_ref,
                y_hbm, xbuf, acc, wgb, wub, wdb, sem_in, sem_out):
    del ve_ref
    v = pl.program_id(0)
    f = pl.program_id(1)
    n_sub = vsub_ref[v]
    row0 = vrow_ref[v]

    def sub_rows(s):
        if isinstance(s, int):
            return pl.ds(s * MOE_SUB, MOE_SUB)
        return pl.ds(pl.multiple_of(s * MOE_SUB, MOE_SUB), MOE_SUB)

    def slab_copies(src_of, dst_of, sem, count):
        def each(method):
            def body(s, carry):
                getattr(pltpu.make_async_copy(src_of(s), dst_of(s), sem), method)()
                return carry
            return lambda: lax.fori_loop(0, count, body, 0)
        return each("start"), each("wait")

    def buf_rows(ref):
        return lambda s: ref.at[sub_rows(s)]

    def hbm_rows(ref, base):
        return lambda s: ref.at[pl.ds(pl.multiple_of(base + s * MOE_SUB, MOE_SUB), MOE_SUB)]

    def zero_acc(s, carry):
        acc[sub_rows(s), :] = jnp.zeros((MOE_SUB, acc.shape[1]), F32)
        return carry

    @pl.when(jnp.logical_and(v == 0, f == 0))
    def _():
        zero_acc(0, 0)
        start, wait = slab_copies(lambda s: acc.at[sub_rows(0)], hbm_rows(y_hbm, tail_ref[0]),
                                  sem_out, tail_ref[1])
        start()
        wait()

    @pl.when(n_sub > 0)
    def _():
        @pl.when(f == 0)
        def _():
            start, wait = slab_copies(hbm_rows(xs_hbm, row0), buf_rows(xbuf), sem_in, n_sub)
            start()
            lax.fori_loop(0, n_sub, zero_acc, 0)
            wait()

        wgb[...] = wg_ref[...].astype(BF16)
        wub[...] = wu_ref[...].astype(BF16)
        wdb[...] = wd_ref[...].astype(BF16)

        def sub_tile(s):
            r = sub_rows(s)
            xt = xbuf[r, :]
            gate = jnp.dot(xt, wgb[...], preferred_element_type=F32)
            up = jnp.dot(xt, wub[...], preferred_element_type=F32)
            hid = (gate * jax.nn.sigmoid(gate) * up).astype(BF16)
            acc[r, :] += jnp.dot(hid, wdb[...], preferred_element_type=F32)

        @pl.when(n_sub == MOE_SUBS_PER_VISIT)
        def _():
            for s in range(MOE_SUBS_PER_VISIT):
                sub_tile(s)

        @pl.when(n_sub < MOE_SUBS_PER_VISIT)
        def _():
            def group(i, carry):
                for j in range(MOE_SUB_UNROLL):
                    sub_tile(i * MOE_SUB_UNROLL + j)
                return carry

            def single(s, carry):
                sub_tile(s)
                return carry

            n_group = n_sub // MOE_SUB_UNROLL
            lax.fori_loop(0, n_group, group, 0)
            lax.fori_loop(n_group * MOE_SUB_UNROLL, n_sub, single, 0)

        @pl.when(f == pl.num_programs(1) - 1)
        def _():
            start, wait = slab_copies(buf_rows(acc), hbm_rows(y_hbm, row0), sem_out, n_sub)
            start()
            wait()


def _moe_experts(xs, visit_expert, visit_row, visit_subs, tail, wg, wu, wd):
    p_rows = xs.shape[0]
    _, d, hidden = wg.shape
    assert xs.shape[1] == d and xs.dtype == BF16
    cols = min(MOE_COLS, hidden)
    assert hidden % cols == 0
    n_f = hidden // cols
    n_visits = visit_expert.shape[0]
    slab = MOE_SUBS_PER_VISIT * MOE_SUB

    def col_tile(v, f, vs):
        return jnp.where(vs[v] > 0, f, n_f - 1)

    est = slab * d * (2 + 4) + 2 * 3 * d * cols * 4 + 3 * d * cols * 2 + 4 * MOE_SUB * d * 4
    grid_spec = pltpu.PrefetchScalarGridSpec(
        num_scalar_prefetch=4,
        grid=(n_visits, n_f),
        in_specs=[
            pl.BlockSpec(memory_space=pl.ANY),
            pl.BlockSpec((None, d, cols), lambda v, f, ve, vr, vs, tl: (ve[v], 0, col_tile(v, f, vs))),
            pl.BlockSpec((None, d, cols), lambda v, f, ve, vr, vs, tl: (ve[v], 0, col_tile(v, f, vs))),
            pl.BlockSpec((None, cols, d), lambda v, f, ve, vr, vs, tl: (ve[v], col_tile(v, f, vs), 0)),
        ],
        out_specs=pl.BlockSpec(memory_space=pl.ANY),
        scratch_shapes=[
            pltpu.VMEM((slab, d), BF16),
            pltpu.VMEM((slab, d), F32),
            pltpu.VMEM((d, cols), BF16),
            pltpu.VMEM((d, cols), BF16),
            pltpu.VMEM((cols, d), BF16),
            pltpu.SemaphoreType.DMA(()),
            pltpu.SemaphoreType.DMA(()),
        ],
    )
    return pl.pallas_call(
        _moe_kernel,
        grid_spec=grid_spec,
        out_shape=jax.ShapeDtypeStruct((p_rows, d), F32),
        compiler_params=pltpu.CompilerParams(
            dimension_semantics=("arbitrary", "arbitrary"), vmem_limit_bytes=_vmem_limit(est)),
        name="moe_experts",
    )(visit_expert, visit_row, visit_subs, tail, xs, wg, wu, wd)


def _combine_kernel(pos_ref, xp_ref, xs_ref, gate_ref, g_ref, y_hbm, op_ref, os_ref, ybuf, sems,
                    *, rows, prompt_tiles):
    i = pl.program_id(0)
    n = pl.num_programs(0)
    slot = i % 2

    def issue(tile_slot, pos_row):
        def body(j, carry):
            for k in range(TOP_K):
                pltpu.make_async_copy(y_hbm.at[pl.ds(pos_ref[pos_row, 0, TOP_K * j + k], 1)],
                                      ybuf.at[tile_slot, k, pl.ds(j, 1)],
                                      sems.at[tile_slot]).start()
            return carry

        lax.fori_loop(0, rows, body, 0, unroll=ROW_DMA_UNROLL)

    @pl.when(i == 0)
    def _():
        issue(0, 0)

    @pl.when(i + 1 < n)
    def _():
        issue(1 - slot, 1)

    pltpu.make_async_copy(ybuf.at[slot], ybuf.at[slot], sems.at[slot]).wait()
    gate = gate_ref[...]
    x = jnp.where(i < prompt_tiles, xp_ref[...], xs_ref[...])
    out = x + gate[:, 0:1] * ybuf[slot, 0] + gate[:, 1:2] * ybuf[slot, 1]
    out = _rmsnorm(out, g_ref[...])

    @pl.when(i < prompt_tiles)
    def _():
        op_ref[...] = out

    @pl.when(i >= prompt_tiles)
    def _():
        os_ref[...] = out


def _combine(x3_p, x3_s, y_sorted, pos_tok, gate_tok, g_final):
    m_p, d = x3_p.shape
    m_s = x3_s.shape[0]
    rows = COMBINE_ROWS
    assert m_p % rows == 0 and m_s == rows
    prompt_tiles = m_p // rows
    tiles = prompt_tiles + 1
    pos_pair = _next_tile_table(pos_tok.reshape(tiles, rows * TOP_K))
    est = 2 * TOP_K * rows * d * 4 + 8 * rows * d * 4
    return pl.pallas_call(
        functools.partial(_combine_kernel, rows=rows, prompt_tiles=prompt_tiles),
        grid=(tiles,),
        in_specs=[
            pl.BlockSpec((2, 1, rows * TOP_K), lambda i: (i, 0, 0), memory_space=pltpu.SMEM),
            pl.BlockSpec((rows, d), lambda i: (jnp.minimum(i, prompt_tiles - 1), 0)),
            pl.BlockSpec((rows, d), lambda i: (0, 0)),
            pl.BlockSpec((rows, TOP_K), lambda i: (i, 0)),
            pl.BlockSpec((1, d), lambda i: (0, 0)),
            pl.BlockSpec(memory_space=pl.ANY),
        ],
        out_specs=(
            pl.BlockSpec((rows, d), lambda i: (jnp.minimum(i, prompt_tiles - 1), 0)),
            pl.BlockSpec((rows, d), lambda i: (0, 0)),
        ),
        out_shape=(jax.ShapeDtypeStruct((m_p, d), F32), jax.ShapeDtypeStruct((m_s, d), F32)),
        scratch_shapes=[pltpu.VMEM((2, TOP_K, rows, d), F32), pltpu.SemaphoreType.DMA((2,))],
        compiler_params=pltpu.CompilerParams(
            dimension_semantics=("arbitrary",), vmem_limit_bytes=_vmem_limit(est)),
        name="moe_combine",
    )(pos_pair, x3_p, x3_s, gate_tok, g_final, y_sorted)


def _routing_tables(sel, gates):
    m, n_exp = sel.shape
    p_rows = _round_up(m * TOP_K, MOE_SUB) + n_exp * MOE_SUB
    counts = jnp.sum(sel, axis=0)
    subs = (counts + MOE_SUB - 1) // MOE_SUB
    group_rows = subs * MOE_SUB
    group_start = jnp.cumsum(group_rows) - group_rows
    rank = jnp.cumsum(sel, axis=0) - sel
    pos_full = group_start[None, :] + rank
    lane = jnp.arange(n_exp, dtype=jnp.int32)[None, :]
    first = jnp.argmax(sel, axis=1).astype(jnp.int32)
    second = jnp.argmax(jnp.where(lane == first[:, None], 0, sel), axis=1).astype(jnp.int32)
    order = jnp.stack([first, second], axis=1)
    pos_tok = jnp.take_along_axis(pos_full, order, axis=1).astype(jnp.int32)
    gate_tok = jnp.take_along_axis(gates, order, axis=1)
    token = jnp.broadcast_to(jnp.arange(m, dtype=jnp.int32)[:, None], (m, TOP_K))
    src = jnp.zeros((p_rows,), jnp.int32).at[pos_tok.reshape(-1)].set(token.reshape(-1))

    n_visits = n_exp + (p_rows // MOE_SUB) // MOE_SUBS_PER_VISIT
    visits_per = (subs + MOE_SUBS_PER_VISIT - 1) // MOE_SUBS_PER_VISIT
    visit_end = jnp.cumsum(visits_per)
    vid = jnp.arange(n_visits, dtype=jnp.int32)
    valid = vid < visit_end[-1]
    owner = jnp.minimum(vid, visit_end[-1] - 1)
    expert = jnp.sum(visit_end[None, :] <= owner[:, None], axis=1).astype(jnp.int32)
    j = vid - (visit_end - visits_per)[expert]
    visit_subs = jnp.where(valid, jnp.clip(subs[expert] - j * MOE_SUBS_PER_VISIT, 0, MOE_SUBS_PER_VISIT), 0)
    visit_row = jnp.where(valid, group_start[expert] + j * MOE_SUBS_PER_VISIT * MOE_SUB, 0)
    total = jnp.sum(group_rows)
    tail = jnp.stack([total, (p_rows - total) // MOE_SUB]).astype(jnp.int32)
    return (src, pos_tok, gate_tok, expert, visit_row.astype(jnp.int32),
            visit_subs.astype(jnp.int32), tail)


def _gate_tables(w_s, b_s, seq_len, width):
    heads = w_s.shape[0]
    length = min(seq_len, GATE_CHUNK)
    pos = jnp.arange(length)
    mask = (pos[:, None] // CAUSAL_CHUNK) >= (pos[None, :] // CAUSAL_CHUNK)
    w = jnp.where(mask[None], w_s[:, :length, :length], 0.0)
    reps = GATE_CHUNK // length
    if reps > 1:
        w = jnp.einsum("ab,hts->hatbs", jnp.eye(reps, dtype=w.dtype), w).reshape(
            heads, GATE_CHUNK, GATE_CHUNK)
    bias = jnp.tile(b_s[:, :length].T, (reps, 1))
    bias = jnp.repeat(bias, width // heads, axis=1)
    return w.astype(BF16), bias


def kernel(x_prompt, x_sample, cache_conv_a, cache_conv_c, l0_norm_mix, l0_w_in, l0_b_in, l0_conv_w, l0_conv_b, l0_ln_a_g, l0_ln_a_b, l0_ln_v_g, l0_ln_v_b, l0_w_s, l0_b_s, l0_w_out, l0_norm_ffn, l0_ffn_gate, l0_ffn_up, l0_ffn_down, l1_norm_mix, l1_w_in, l1_conv_w, l1_w_out, l1_norm_ffn, l1_router, l1_moe_gate, l1_moe_up, l1_moe_down, final_norm):
    n_p, t_p, d = x_prompt.shape
    n_s, t_s, _ = x_sample.shape
    m_p, m_s = n_p * t_p, n_s * t_s
    a_width = l0_conv_b.shape[0]
    row = lambda vec: vec.reshape(1, -1)
    xp = x_prompt.reshape(m_p, d)
    xs = x_sample.reshape(m_s, d)

    l0_head = (row(l0_norm_mix), l0_w_in.astype(BF16), row(l0_b_in), l0_conv_w, row(l0_conv_b),
               row(l0_ln_a_g), row(l0_ln_a_b), row(l0_ln_v_g), row(l0_ln_v_b))
    wout0 = l0_w_out.astype(BF16)
    zero_a = jnp.zeros((n_p,) + cache_conv_a.shape[1:], F32)
    xp, conv_a_prompt, _ = _l0_mixer(
        xp, zero_a, l0_head + _gate_tables(l0_w_s, l0_b_s, t_p, a_width) + (wout0,),
        n_seq=n_p, seq_len=t_p, keep_v=False)
    xs, conv_a_sample, v_sample = _l0_mixer(
        xs, cache_conv_a, l0_head + _gate_tables(l0_w_s, l0_b_s, t_s, a_width) + (wout0,),
        n_seq=n_s, seq_len=t_s, keep_v=True)

    ffn_w = (row(l0_norm_ffn), l0_ffn_gate.astype(BF16), l0_ffn_up.astype(BF16),
             l0_ffn_down.astype(BF16))
    xp = _ffn(xp, *ffn_w)
    xs = _ffn(xs, *ffn_w)

    l1_w = (row(l1_norm_mix), l1_w_in.astype(BF16), l1_conv_w, l1_w_out.astype(BF16),
            row(l1_norm_ffn), l1_router)
    zero_c = jnp.zeros((n_p,) + cache_conv_c.shape[1:], F32)
    xp, xn_p, gates_p, sel_p, conv_c_prompt = _l1_mixer(xp, zero_c, l1_w, n_seq=n_p, seq_len=t_p)
    xs, xn_s, gates_s, sel_s, conv_c_sample = _l1_mixer(xs, cache_conv_c, l1_w, n_seq=n_s, seq_len=t_s)

    sel = jnp.concatenate([sel_p, sel_s], axis=0)
    gates = jnp.concatenate([gates_p, gates_s], axis=0)
    src, pos_tok, gate_tok, v_expert, v_row, v_subs, tail = _routing_tables(sel, gates)
    x_sorted = _gather_rows(xn_p, xn_s, src, rows=MOE_SUB, out_dtype=BF16)
    y_sorted = _moe_experts(x_sorted, v_expert, v_row, v_subs, tail,
                            l1_moe_gate, l1_moe_up, l1_moe_down)
    y_p, y_s = _combine(xp, xs, y_sorted, pos_tok, gate_tok, row(final_norm))

    return (y_p.reshape(n_p, t_p, d), y_s.reshape(n_s, t_s, d), conv_a_prompt, conv_a_sample,
            v_sample.reshape(n_s, t_s, a_width), conv_c_prompt, conv_c_sample)
```

```python
import functools

import jax
import jax.numpy as jnp
from jax import lax
from jax.experimental import pallas as pl
from jax.experimental.pallas import tpu as pltpu

EPS = 1e-5
CAUSAL_CHUNK = 64
GATE_CHUNK = 128
TOP_K = 2

V7X_SUBLANES = 8
V7X_SCOPED_VMEM_BYTES = 60000 * 1024

MIX_ROWS = 256
CONV_ROW_BLOCK = 32
FFN_ROWS = 512
FFN_COLS = 512
MOE_SUB = 256
MOE_SUBS_PER_VISIT = 9
MOE_SUB_UNROLL = 3
MOE_COLS = 256
COMBINE_ROWS = 128
ROW_DMA_UNROLL = 8
N_DMA_QUEUES = 2

F32 = jnp.float32
BF16 = jnp.bfloat16


def _round_up(n, m):
    return -(-n // m) * m


def _vmem_limit(estimate_bytes):
    return int(min(V7X_SCOPED_VMEM_BYTES, estimate_bytes + (8 << 20)))


def _rmsnorm(x, g):
    return x * lax.rsqrt(jnp.mean(x * x, axis=-1, keepdims=True) + EPS) * g


def _layernorm(x, g, b):
    mu = jnp.mean(x, axis=-1, keepdims=True)
    xc = x - mu
    return xc * lax.rsqrt(jnp.mean(xc * xc, axis=-1, keepdims=True) + EPS) * g + b


def _dot_bf16x3(a, b):
    a_hi = a.astype(BF16)
    a_lo = (a - a_hi.astype(F32)).astype(BF16)
    b_hi = b.astype(BF16)
    b_lo = (b - b_hi.astype(F32)).astype(BF16)
    dot = functools.partial(jnp.dot, preferred_element_type=F32)
    return dot(a_hi, b_hi) + (dot(a_hi, b_lo) + dot(a_lo, b_hi))


def _resident(shape):
    nd = len(shape)
    return pl.BlockSpec(shape, lambda *_: (0,) * nd, pipeline_mode=pl.Buffered(1))


def _mixer_tiling(n_seq, seq_len):
    seg_len = min(seq_len, MIX_ROWS)
    n_seg = 1 if seq_len >= MIX_ROWS else n_seq
    tiles_per_seq = seq_len // seg_len if n_seg == 1 else 1
    assert seq_len % seg_len == 0 and (seg_len * n_seg) % (2 * V7X_SUBLANES) == 0
    return seg_len, n_seg, tiles_per_seq, n_seq // n_seg


def _conv_geometry(ksize, seg_len):
    pad = _round_up(ksize - 1, V7X_SUBLANES)
    off = pad - (ksize - 1)
    residues = sorted({(off + k) % V7X_SUBLANES for k in range(ksize)} - {0})
    return pad, off, residues, pad + seg_len - V7X_SUBLANES


def _causal_conv_segments(src, state0_ref, ext_ref, shift_ref, dst_ref, state_out_ref, cw_ref,
                          bias, *, seg_len, n_seg, first_tile):
    ksize = cw_ref.shape[0]
    width = src.shape[-1]
    pad, off, residues, n_shift = _conv_geometry(ksize, seg_len)
    for s in range(n_seg):
        @pl.when(first_tile)
        def _():
            ext_ref[off:pad, :] = state0_ref[s]

        ext_ref[pad:pad + seg_len, :] = src[s * seg_len:(s + 1) * seg_len]
        for i, r in enumerate(residues):
            shift_ref[i] = ext_ref[r:r + n_shift, :]
        for r0 in range(0, seg_len, CONV_ROW_BLOCK):
            rb = min(CONV_ROW_BLOCK, seg_len - r0)
            acc = jnp.broadcast_to(bias, (rb, width))
            for k in range(ksize):
                q, r = divmod(off + k, V7X_SUBLANES)
                lo = q * V7X_SUBLANES + r0
                if r == 0:
                    tap = ext_ref[lo:lo + rb, :]
                else:
                    tap = shift_ref[residues.index(r), lo:lo + rb, :]
                acc = acc + cw_ref[k:k + 1, :] * tap
            dst_ref[s * seg_len + r0:s * seg_len + r0 + rb, :] = acc
        state_out_ref[s] = ext_ref[off + seg_len:pad + seg_len, :]
        if n_seg == 1:
            ext_ref[0:pad, :] = ext_ref[seg_len:seg_len + pad, :]


def _l0_mixer_kernel(x_ref, state0_ref, g_ref, win_ref, bin_ref, cw_ref, cb_ref,
                     lag_ref, lab_ref, lvg_ref, lvb_ref, ws_ref, bs_ref, wout_ref,
                     x1_ref, state_ref, v_ref, ext_ref, shift_ref, y_ref, ab_ref,
                     *, seg_len, n_seg):
    c = cb_ref.shape[-1]
    heads, lc, _ = ws_ref.shape
    hd = c // heads
    rows = seg_len * n_seg

    x = x_ref[...]
    h = _rmsnorm(x, g_ref[...]).astype(BF16)
    z = jnp.dot(h, win_ref[...], preferred_element_type=F32) + bin_ref[...]
    a = z[:, :c] * jax.nn.sigmoid(z[:, c:2 * c])
    u = z[:, 2 * c:3 * c]
    v = z[:, 3 * c:]

    _causal_conv_segments(a, state0_ref, ext_ref, shift_ref, y_ref, state_ref, cw_ref,
                          cb_ref[...], seg_len=seg_len, n_seg=n_seg,
                          first_tile=pl.program_id(1) == 0)
    a_act = _layernorm(y_ref[...], lag_ref[...], lab_ref[...])
    ab_ref[:, :c] = (a_act * jax.nn.sigmoid(a_act)).astype(BF16)

    vn = _layernorm(v, lvg_ref[...], lvb_ref[...])
    v_ref[...] = vn
    vb = vn.astype(BF16)
    for ci in range(rows // lc):
        r = slice(ci * lc, (ci + 1) * lc)
        for hh in range(heads):
            cs = slice(hh * hd, (hh + 1) * hd)
            s = jnp.dot(ws_ref[hh], vb[r, cs], preferred_element_type=F32) + bs_ref[:, cs]
            ab_ref[r, c + hh * hd:c + (hh + 1) * hd] = (u[r, cs] * s).astype(BF16)

    x1_ref[...] = x + jnp.dot(ab_ref[...], wout_ref[...], preferred_element_type=F32)


def _l0_mixer(x, state0, weights, *, n_seq, seq_len, keep_v):
    g, win, b_in, cw, cb, lag, lab, lvg, lvb, ws, bs, wout = weights
    m, d = x.shape
    c = cb.shape[-1]
    ksize = cw.shape[0]
    seg_len, n_seg, tiles_per_seq, groups = _mixer_tiling(n_seq, seq_len)
    rows = seg_len * n_seg
    assert rows % GATE_CHUNK == 0
    pad, _, residues, n_shift = _conv_geometry(ksize, seg_len)

    row_map = lambda i, t: (i * tiles_per_seq + t, 0)
    state_spec = pl.BlockSpec((n_seg, ksize - 1, c), lambda i, t: (i, 0, 0))
    in_specs = [pl.BlockSpec((rows, d), row_map), state_spec] + [_resident(w.shape) for w in weights]
    out_shape = (
        jax.ShapeDtypeStruct((m, d), F32),
        jax.ShapeDtypeStruct((n_seq, ksize - 1, c), F32),
        jax.ShapeDtypeStruct((m if keep_v else rows, c), F32),
    )
    out_specs = (
        pl.BlockSpec((rows, d), row_map),
        state_spec,
        pl.BlockSpec((rows, c), row_map if keep_v else (lambda i, t: (0, 0))),
    )
    est = (2 * (win.size + wout.size) + 4 * rows * d * 4 + 4 * rows * 4 * c * 4
           + 4 * rows * c * 4 + (1 + len(residues)) * (pad + seg_len) * c * 4)
    return pl.pallas_call(
        functools.partial(_l0_mixer_kernel, seg_len=seg_len, n_seg=n_seg),
        grid=(groups, tiles_per_seq),
        in_specs=in_specs,
        out_specs=out_specs,
        out_shape=out_shape,
        scratch_shapes=[
            pltpu.VMEM((pad + seg_len, c), F32),
            pltpu.VMEM((len(residues), n_shift, c), F32),
            pltpu.VMEM((rows, c), F32),
            pltpu.VMEM((rows, 2 * c), BF16),
        ],
        compiler_params=pltpu.CompilerParams(
            dimension_semantics=("arbitrary", "arbitrary"), vmem_limit_bytes=_vmem_limit(est)),
        name="l0_mixer",
    )(x, state0, *weights)


def _ffn_kernel(x_ref, g_ref, wg_ref, wu_ref, wd_ref, o_ref, xn_ref, acc_ref):
    f = pl.program_id(1)

    @pl.when(f == 0)
    def _():
        xn_ref[...] = _rmsnorm(x_ref[...], g_ref[...]).astype(BF16)
        acc_ref[...] = jnp.zeros_like(acc_ref)

    xn = xn_ref[...]
    gate = jnp.dot(xn, wg_ref[...], preferred_element_type=F32)
    up = jnp.dot(xn, wu_ref[...], preferred_element_type=F32)
    hid = (gate * jax.nn.sigmoid(gate) * up).astype(BF16)
    acc_ref[...] += jnp.dot(hid, wd_ref[...], preferred_element_type=F32)

    @pl.when(f == pl.num_programs(1) - 1)
    def _():
        o_ref[...] = x_ref[...] + acc_ref[...]


def _ffn(x, g, wg, wu, wd):
    m, d = x.shape
    hidden = wg.shape[1]
    rows = min(FFN_ROWS, m)
    cols = min(FFN_COLS, hidden)
    assert m % rows == 0 and hidden % cols == 0 and rows % (2 * V7X_SUBLANES) == 0
    est = 4 * rows * d * 4 + rows * d * 4 + rows * d * 2 + 2 * 3 * d * cols * 2 + 3 * rows * cols * 4
    return pl.pallas_call(
        _ffn_kernel,
        grid=(m // rows, hidden // cols),
        in_specs=[
            pl.BlockSpec((rows, d), lambda i, f: (i, 0)),
            pl.BlockSpec((1, d), lambda i, f: (0, 0)),
            pl.BlockSpec((d, cols), lambda i, f: (0, f)),
            pl.BlockSpec((d, cols), lambda i, f: (0, f)),
            pl.BlockSpec((cols, d), lambda i, f: (f, 0)),
        ],
        out_specs=pl.BlockSpec((rows, d), lambda i, f: (i, 0)),
        out_shape=jax.ShapeDtypeStruct((m, d), F32),
        scratch_shapes=[pltpu.VMEM((rows, d), BF16), pltpu.VMEM((rows, d), F32)],
        compiler_params=pltpu.CompilerParams(
            dimension_semantics=("arbitrary", "arbitrary"), vmem_limit_bytes=_vmem_limit(est)),
        name="l0_ffn",
    )(x, g, wg, wu, wd)


def _top2_gates(logits):
    n_exp = logits.shape[-1]
    lane = lax.broadcasted_iota(jnp.int32, logits.shape, 1)
    m1 = jnp.max(logits, axis=-1, keepdims=True)
    i1 = jnp.min(jnp.where(logits == m1, lane, n_exp), axis=-1, keepdims=True)
    sel1 = lane == i1
    rest = jnp.where(sel1, -jnp.inf, logits)
    m2 = jnp.max(rest, axis=-1, keepdims=True)
    i2 = jnp.min(jnp.where(rest == m2, lane, n_exp), axis=-1, keepdims=True)
    sel2 = lane == i2
    e2 = jnp.exp(m2 - m1)
    denom = 1.0 + e2
    gates = jnp.where(sel1, 1.0 / denom, 0.0) + jnp.where(sel2, e2 / denom, 0.0)
    return gates, (sel1 | sel2).astype(jnp.int32)


def _l1_mixer_kernel(x_ref, state0_ref, g_ref, win_ref, cw_ref, wout_ref, gf_ref, rt_ref,
                     x3_ref, xn_ref, gates_ref, sel_ref, state_ref, ext_ref, shift_ref, y_ref,
                     *, seg_len, n_seg):
    cw = cw_ref.shape[-1]
    x = x_ref[...]
    h = _rmsnorm(x, g_ref[...]).astype(BF16)
    z = jnp.dot(h, win_ref[...], preferred_element_type=F32)
    b_g = z[:, :cw]
    p = z[:, cw:2 * cw] * z[:, 2 * cw:]
    _causal_conv_segments(p, state0_ref, ext_ref, shift_ref, y_ref, state_ref, cw_ref,
                          jnp.zeros((1, cw), F32),
                          seg_len=seg_len, n_seg=n_seg, first_tile=pl.program_id(1) == 0)
    q = (b_g * y_ref[...]).astype(BF16)
    x3 = x + jnp.dot(q, wout_ref[...], preferred_element_type=F32)
    x3_ref[...] = x3
    xn = _rmsnorm(x3, gf_ref[...])
    xn_ref[...] = xn
    gates, sel = _top2_gates(_dot_bf16x3(xn, rt_ref[...]))
    gates_ref[...] = gates
    sel_ref[...] = sel


def _l1_mixer(x, state0, weights, *, n_seq, seq_len):
    g, win, cw, wout, gf, router = weights
    m, d = x.shape
    width = cw.shape[-1]
    ksize = cw.shape[0]
    n_exp = router.shape[-1]
    seg_len, n_seg, tiles_per_seq, groups = _mixer_tiling(n_seq, seq_len)
    rows = seg_len * n_seg
    pad, _, residues, n_shift = _conv_geometry(ksize, seg_len)

    row_map = lambda i, t: (i * tiles_per_seq + t, 0)
    state_spec = pl.BlockSpec((n_seg, ksize - 1, width), lambda i, t: (i, 0, 0))
    in_specs = [pl.BlockSpec((rows, d), row_map), state_spec] + [_resident(w.shape) for w in weights]
    out_shape = (
        jax.ShapeDtypeStruct((m, d), F32),
        jax.ShapeDtypeStruct((m, d), F32),
        jax.ShapeDtypeStruct((m, n_exp), F32),
        jax.ShapeDtypeStruct((m, n_exp), jnp.int32),
        jax.ShapeDtypeStruct((n_seq, ksize - 1, width), F32),
    )
    out_specs = (
        pl.BlockSpec((rows, d), row_map), pl.BlockSpec((rows, d), row_map),
        pl.BlockSpec((rows, n_exp), row_map), pl.BlockSpec((rows, n_exp), row_map),
        state_spec,
    )
    est = (2 * (win.size + wout.size) + 6 * rows * d * 4 + 3 * rows * 3 * width * 4
           + 3 * rows * width * 4)
    return pl.pallas_call(
        functools.partial(_l1_mixer_kernel, seg_len=seg_len, n_seg=n_seg),
        grid=(groups, tiles_per_seq),
        in_specs=in_specs,
        out_specs=out_specs,
        out_shape=out_shape,
        scratch_shapes=[
            pltpu.VMEM((pad + seg_len, width), F32),
            pltpu.VMEM((len(residues), n_shift, width), F32),
            pltpu.VMEM((rows, width), F32),
        ],
        compiler_params=pltpu.CompilerParams(
            dimension_semantics=("arbitrary", "arbitrary"), vmem_limit_bytes=_vmem_limit(est)),
        name="l1_mixer",
    )(x, state0, *weights)


def _next_tile_table(table):
    tiles, n = table.shape
    return jnp.stack([table, jnp.roll(table, -1, axis=0)], axis=1).reshape(2 * tiles, 1, n)


def _gather_rows_kernel(src_ref, xa_hbm, xb_hbm, o_ref, buf, sems, *, rows):
    i = pl.program_id(0)
    n = pl.num_programs(0)
    slot = i % 2
    n_a = xa_hbm.shape[0]

    def issue(tile_slot, table_row):
        def body(jj, carry):
            for prio in range(N_DMA_QUEUES):
                j = jj * N_DMA_QUEUES + prio
                t = src_ref[table_row, 0, j]
                dst = buf.at[tile_slot, pl.ds(j, 1)]

                @pl.when(t < n_a)
                def _():
                    pltpu.make_async_copy(xa_hbm.at[pl.ds(t, 1)], dst,
                                          sems.at[tile_slot]).start(priority=prio)

                @pl.when(t >= n_a)
                def _():
                    pltpu.make_async_copy(xb_hbm.at[pl.ds(t - n_a, 1)], dst,
                                          sems.at[tile_slot]).start(priority=prio)

            return carry

        lax.fori_loop(0, rows // N_DMA_QUEUES, body, 0, unroll=ROW_DMA_UNROLL // N_DMA_QUEUES)

    @pl.when(i == 0)
    def _():
        issue(0, 0)

    @pl.when(i + 1 < n)
    def _():
        issue(1 - slot, 1)

    pltpu.make_async_copy(buf.at[slot], buf.at[slot], sems.at[slot]).wait()
    o_ref[...] = buf[slot].astype(o_ref.dtype)


def _gather_rows(xa, xb, src, *, rows, out_dtype):
    n_out = src.shape[0]
    width = xa.shape[1]
    assert n_out % rows == 0 and xa.shape[1:] == xb.shape[1:] and xa.dtype == xb.dtype
    tiles = n_out // rows
    return pl.pallas_call(
        functools.partial(_gather_rows_kernel, rows=rows),
        grid=(tiles,),
        in_specs=[
            pl.BlockSpec((2, 1, rows), lambda i: (i, 0, 0), memory_space=pltpu.SMEM),
            pl.BlockSpec(memory_space=pl.ANY),
            pl.BlockSpec(memory_space=pl.ANY),
        ],
        out_specs=pl.BlockSpec((rows, width), lambda i: (i, 0)),
        out_shape=jax.ShapeDtypeStruct((n_out, width), out_dtype),
        scratch_shapes=[pltpu.VMEM((2, rows, width), xa.dtype), pltpu.SemaphoreType.DMA((2,))],
        compiler_params=pltpu.CompilerParams(dimension_semantics=("arbitrary",)),
        name="moe_gather",
    )(_next_tile_table(src.reshape(tiles, rows)), xa, xb)


def _moe_kernel(ve_ref, vrow_ref, vsub_ref, tail_ref, xs_hbm, wg_ref, wu_ref, wd_ref,
                y_hbm, xbuf, acc, wgb, wub, wdb, sem_in, sem_out):
    del ve_ref
    v = pl.program_id(0)
    f = pl.program_id(1)
    n_sub = vsub_ref[v]
    row0 = vrow_ref[v]

    def sub_rows(s):
        if isinstance(s, int):
            return pl.ds(s * MOE_SUB, MOE_SUB)
        return pl.ds(pl.multiple_of(s * MOE_SUB, MOE_SUB), MOE_SUB)

    def slab_copies(src_of, dst_of, sem, count):
        def each(method):
            def body(s, carry):
                getattr(pltpu.make_async_copy(src_of(s), dst_of(s), sem), method)()
                return carry
            return lambda: lax.fori_loop(0, count, body, 0)
        return each("start"), each("wait")

    def buf_rows(ref):
        return lambda s: ref.at[sub_rows(s)]

    def hbm_rows(ref, base):
        return lambda s: ref.at[pl.ds(pl.multiple_of(base + s * MOE_SUB, MOE_SUB), MOE_SUB)]

    def zero_acc(s, carry):
        acc[sub_rows(s), :] = jnp.zeros((MOE_SUB, acc.shape[1]), F32)
        return carry

    @pl.when(jnp.logical_and(v == 0, f == 0))
    def _():
        zero_acc(0, 0)
        start, wait = slab_copies(lambda s: acc.at[sub_rows(0)], hbm_rows(y_hbm, tail_ref[0]),
                                  sem_out, tail_ref[1])
        start()
        wait()

    @pl.when(n_sub > 0)
    def _():
        @pl.when(f == 0)
        def _():
            start, wait = slab_copies(hbm_rows(xs_hbm, row0), buf_rows(xbuf), sem_in, n_sub)
            start()
            lax.fori_loop(0, n_sub, zero_acc, 0)
            wait()

        def sub_tile(s, wg, wu, wd):
            r = sub_rows(s)
            xt = xbuf[r, :]
            gate = jnp.dot(xt, wg, preferred_element_type=F32)
            up = jnp.dot(xt, wu, preferred_element_type=F32)
            hid = (gate * jax.nn.sigmoid(gate) * up).astype(BF16)
            acc[r, :] += jnp.dot(hid, wd, preferred_element_type=F32)

        @pl.when(n_sub == MOE_SUBS_PER_VISIT)
        def _():
            wg = wg_ref[...].astype(BF16)
            wu = wu_ref[...].astype(BF16)
            wd = wd_ref[...].astype(BF16)
            for s in range(MOE_SUBS_PER_VISIT):
                sub_tile(s, wg, wu, wd)

        @pl.when(n_sub < MOE_SUBS_PER_VISIT)
        def _():
            wgb[...] = wg_ref[...].astype(BF16)
            wub[...] = wu_ref[...].astype(BF16)
            wdb[...] = wd_ref[...].astype(BF16)

            def group(i, carry):
                for j in range(MOE_SUB_UNROLL):
                    sub_tile(i * MOE_SUB_UNROLL + j, wgb[...], wub[...], wdb[...])
                return carry

            def single(s, carry):
                sub_tile(s, wgb[...], wub[...], wdb[...])
                return carry

            n_group = n_sub // MOE_SUB_UNROLL
            lax.fori_loop(0, n_group, group, 0)
            lax.fori_loop(n_group * MOE_SUB_UNROLL, n_sub, single, 0)

        @pl.when(f == pl.num_programs(1) - 1)
        def _():
            start, wait = slab_copies(buf_rows(acc), hbm_rows(y_hbm, row0), sem_out, n_sub)
            start()
            wait()


def _moe_experts(xs, visit_expert, visit_row, visit_subs, tail, n_visits, wg, wu, wd):
    p_rows = xs.shape[0]
    _, d, hidden = wg.shape
    assert xs.shape[1] == d and xs.dtype == BF16
    cols = min(MOE_COLS, hidden)
    assert hidden % cols == 0
    n_f = hidden // cols
    slab = MOE_SUBS_PER_VISIT * MOE_SUB

    def col_tile(v, f, vs):
        return jnp.where(vs[v] > 0, f, n_f - 1)

    est = slab * d * (2 + 4) + 2 * 3 * d * cols * 4 + 3 * d * cols * 2 + 4 * MOE_SUB * d * 4
    grid_spec = pltpu.PrefetchScalarGridSpec(
        num_scalar_prefetch=4,
        grid=(n_visits, n_f),
        in_specs=[
            pl.BlockSpec(memory_space=pl.ANY),
            pl.BlockSpec((None, d, cols), lambda v, f, ve, vr, vs, tl: (ve[v], 0, col_tile(v, f, vs))),
            pl.BlockSpec((None, d, cols), lambda v, f, ve, vr, vs, tl: (ve[v], 0, col_tile(v, f, vs))),
            pl.BlockSpec((None, cols, d), lambda v, f, ve, vr, vs, tl: (ve[v], col_tile(v, f, vs), 0)),
        ],
        out_specs=pl.BlockSpec(memory_space=pl.ANY),
        scratch_shapes=[
            pltpu.VMEM((slab, d), BF16),
            pltpu.VMEM((slab, d), F32),
            pltpu.VMEM((d, cols), BF16),
            pltpu.VMEM((d, cols), BF16),
            pltpu.VMEM((cols, d), BF16),
            pltpu.SemaphoreType.DMA(()),
            pltpu.SemaphoreType.DMA(()),
        ],
    )
    return pl.pallas_call(
        _moe_kernel,
        grid_spec=grid_spec,
        out_shape=jax.ShapeDtypeStruct((p_rows, d), F32),
        compiler_params=pltpu.CompilerParams(
            dimension_semantics=("arbitrary", "arbitrary"), vmem_limit_bytes=_vmem_limit(est)),
        name="moe_experts",
    )(visit_expert, visit_row, visit_subs, tail, xs, wg, wu, wd)


def _combine_kernel(pos_ref, xp_ref, xs_ref, gate_ref, g_ref, y_hbm, op_ref, os_ref, ybuf, sems,
                    *, rows, prompt_tiles):
    i = pl.program_id(0)
    n = pl.num_programs(0)
    slot = i % 2

    def issue(tile_slot, pos_row):
        def body(j, carry):
            for k in range(TOP_K):
                pltpu.make_async_copy(y_hbm.at[pl.ds(pos_ref[pos_row, 0, TOP_K * j + k], 1)],
                                      ybuf.at[tile_slot, k, pl.ds(j, 1)],
                                      sems.at[tile_slot]).start(priority=k % N_DMA_QUEUES)
            return carry

        lax.fori_loop(0, rows, body, 0, unroll=ROW_DMA_UNROLL)

    @pl.when(i == 0)
    def _():
        issue(0, 0)

    @pl.when(i + 1 < n)
    def _():
        issue(1 - slot, 1)

    pltpu.make_async_copy(ybuf.at[slot], ybuf.at[slot], sems.at[slot]).wait()
    gate = gate_ref[...]
    x = jnp.where(i < prompt_tiles, xp_ref[...], xs_ref[...])
    out = x + gate[:, 0:1] * ybuf[slot, 0] + gate[:, 1:2] * ybuf[slot, 1]
    out = _rmsnorm(out, g_ref[...])

    @pl.when(i < prompt_tiles)
    def _():
        op_ref[...] = out

    @pl.when(i >= prompt_tiles)
    def _():
        os_ref[...] = out


def _combine(x3_p, x3_s, y_sorted, pos_tok, gate_tok, g_final):
    m_p, d = x3_p.shape
    m_s = x3_s.shape[0]
    rows = COMBINE_ROWS
    assert m_p % rows == 0 and m_s == rows
    prompt_tiles = m_p // rows
    tiles = prompt_tiles + 1
    pos_pair = _next_tile_table(pos_tok.reshape(tiles, rows * TOP_K))
    est = 2 * TOP_K * rows * d * 4 + 8 * rows * d * 4
    return pl.pallas_call(
        functools.partial(_combine_kernel, rows=rows, prompt_tiles=prompt_tiles),
        grid=(tiles,),
        in_specs=[
            pl.BlockSpec((2, 1, rows * TOP_K), lambda i: (i, 0, 0), memory_space=pltpu.SMEM),
            pl.BlockSpec((rows, d), lambda i: (jnp.minimum(i, prompt_tiles - 1), 0)),
            pl.BlockSpec((rows, d), lambda i: (0, 0)),
            pl.BlockSpec((rows, TOP_K), lambda i: (i, 0)),
            pl.BlockSpec((1, d), lambda i: (0, 0)),
            pl.BlockSpec(memory_space=pl.ANY),
        ],
        out_specs=(
            pl.BlockSpec((rows, d), lambda i: (jnp.minimum(i, prompt_tiles - 1), 0)),
            pl.BlockSpec((rows, d), lambda i: (0, 0)),
        ),
        out_shape=(jax.ShapeDtypeStruct((m_p, d), F32), jax.ShapeDtypeStruct((m_s, d), F32)),
        scratch_shapes=[pltpu.VMEM((2, TOP_K, rows, d), F32), pltpu.SemaphoreType.DMA((2,))],
        compiler_params=pltpu.CompilerParams(
            dimension_semantics=("arbitrary",), vmem_limit_bytes=_vmem_limit(est)),
        name="moe_combine",
    )(pos_pair, x3_p, x3_s, gate_tok, g_final, y_sorted)


def _routing_tables(sel, gates):
    m, n_exp = sel.shape
    p_rows = _round_up(m * TOP_K, MOE_SUB) + n_exp * MOE_SUB
    counts = jnp.sum(sel, axis=0)
    subs = (counts + MOE_SUB - 1) // MOE_SUB
    group_rows = subs * MOE_SUB
    group_start = jnp.cumsum(group_rows) - group_rows
    rank = jnp.cumsum(sel, axis=0) - sel
    pos_full = group_start[None, :] + rank
    lane = jnp.arange(n_exp, dtype=jnp.int32)[None, :]
    first = jnp.argmax(sel, axis=1).astype(jnp.int32)
    second = jnp.argmax(jnp.where(lane == first[:, None], 0, sel), axis=1).astype(jnp.int32)
    order = jnp.stack([first, second], axis=1)
    pos_tok = jnp.take_along_axis(pos_full, order, axis=1).astype(jnp.int32)
    gate_tok = jnp.take_along_axis(gates, order, axis=1)
    token = jnp.broadcast_to(jnp.arange(m, dtype=jnp.int32)[:, None], (m, TOP_K))
    src = jnp.zeros((p_rows,), jnp.int32).at[pos_tok.reshape(-1)].set(token.reshape(-1))

    n_visits = n_exp + (p_rows // MOE_SUB) // MOE_SUBS_PER_VISIT
    visits_per = (subs + MOE_SUBS_PER_VISIT - 1) // MOE_SUBS_PER_VISIT
    visit_end = jnp.cumsum(visits_per)
    vid = jnp.arange(n_visits, dtype=jnp.int32)
    valid = vid < visit_end[-1]
    owner = jnp.minimum(vid, visit_end[-1] - 1)
    expert = jnp.sum(visit_end[None, :] <= owner[:, None], axis=1).astype(jnp.int32)
    j = vid - (visit_end - visits_per)[expert]
    visit_subs = jnp.where(valid, jnp.clip(subs[expert] - j * MOE_SUBS_PER_VISIT, 0, MOE_SUBS_PER_VISIT), 0)
    visit_row = jnp.where(valid, group_start[expert] + j * MOE_SUBS_PER_VISIT * MOE_SUB, 0)
    total = jnp.sum(group_rows)
    tail = jnp.stack([total, (p_rows - total) // MOE_SUB]).astype(jnp.int32)
    return (src, pos_tok, gate_tok, expert, visit_row.astype(jnp.int32),
            visit_subs.astype(jnp.int32), tail, visit_end[-1].astype(jnp.int32))


def _gate_tables(w_s, b_s, seq_len, width):
    heads = w_s.shape[0]
    length = min(seq_len, GATE_CHUNK)
    pos = jnp.arange(length)
    mask = (pos[:, None] // CAUSAL_CHUNK) >= (pos[None, :] // CAUSAL_CHUNK)
    w = jnp.where(mask[None], w_s[:, :length, :length], 0.0)
    reps = GATE_CHUNK // length
    if reps > 1:
        w = jnp.einsum("ab,hts->hatbs", jnp.eye(reps, dtype=w.dtype), w).reshape(
            heads, GATE_CHUNK, GATE_CHUNK)
    bias = jnp.tile(b_s[:, :length].T, (reps, 1))
    bias = jnp.repeat(bias, width // heads, axis=1)
    return w.astype(BF16), bias


def kernel(x_prompt, x_sample, cache_conv_a, cache_conv_c, l0_norm_mix, l0_w_in, l0_b_in, l0_conv_w, l0_conv_b, l0_ln_a_g, l0_ln_a_b, l0_ln_v_g, l0_ln_v_b, l0_w_s, l0_b_s, l0_w_out, l0_norm_ffn, l0_ffn_gate, l0_ffn_up, l0_ffn_down, l1_norm_mix, l1_w_in, l1_conv_w, l1_w_out, l1_norm_ffn, l1_router, l1_moe_gate, l1_moe_up, l1_moe_down, final_norm):
    n_p, t_p, d = x_prompt.shape
    n_s, t_s, _ = x_sample.shape
    m_p, m_s = n_p * t_p, n_s * t_s
    a_width = l0_conv_b.shape[0]
    row = lambda vec: vec.reshape(1, -1)
    xp = x_prompt.reshape(m_p, d)
    xs = x_sample.reshape(m_s, d)

    l0_head = (row(l0_norm_mix), l0_w_in.astype(BF16), row(l0_b_in), l0_conv_w, row(l0_conv_b),
               row(l0_ln_a_g), row(l0_ln_a_b), row(l0_ln_v_g), row(l0_ln_v_b))
    wout0 = l0_w_out.astype(BF16)
    zero_a = jnp.zeros((n_p,) + cache_conv_a.shape[1:], F32)
    xp, conv_a_prompt, _ = _l0_mixer(
        xp, zero_a, l0_head + _gate_tables(l0_w_s, l0_b_s, t_p, a_width) + (wout0,),
        n_seq=n_p, seq_len=t_p, keep_v=False)
    xs, conv_a_sample, v_sample = _l0_mixer(
        xs, cache_conv_a, l0_head + _gate_tables(l0_w_s, l0_b_s, t_s, a_width) + (wout0,),
        n_seq=n_s, seq_len=t_s, keep_v=True)

    ffn_w = (row(l0_norm_ffn), l0_ffn_gate.astype(BF16), l0_ffn_up.astype(BF16),
             l0_ffn_down.astype(BF16))
    xp = _ffn(xp, *ffn_w)
    xs = _ffn(xs, *ffn_w)

    l1_w = (row(l1_norm_mix), l1_w_in.astype(BF16), l1_conv_w, l1_w_out.astype(BF16),
            row(l1_norm_ffn), l1_router)
    zero_c = jnp.zeros((n_p,) + cache_conv_c.shape[1:], F32)
    xp, xn_p, gates_p, sel_p, conv_c_prompt = _l1_mixer(xp, zero_c, l1_w, n_seq=n_p, seq_len=t_p)
    xs, xn_s, gates_s, sel_s, conv_c_sample = _l1_mixer(xs, cache_conv_c, l1_w, n_seq=n_s, seq_len=t_s)

    sel = jnp.concatenate([sel_p, sel_s], axis=0)
    gates = jnp.concatenate([gates_p, gates_s], axis=0)
    src, pos_tok, gate_tok, v_expert, v_row, v_subs, tail, n_visits = _routing_tables(sel, gates)
    x_sorted = _gather_rows(xn_p, xn_s, src, rows=MOE_SUB, out_dtype=BF16)
    y_sorted = _moe_experts(x_sorted, v_expert, v_row, v_subs, tail, n_visits,
                            l1_moe_gate, l1_moe_up, l1_moe_down)
    y_p, y_s = _combine(xp, xs, y_sorted, pos_tok, gate_tok, row(final_norm))

    return (y_p.reshape(n_p, t_p, d), y_s.reshape(n_s, t_s, d), conv_a_prompt, conv_a_sample,
            v_sample.reshape(n_s, t_s, a_width), conv_c_prompt, conv_c_sample)
```

```python
import functools

import jax
import jax.numpy as jnp
from jax import lax
from jax.experimental import pallas as pl
from jax.experimental.pallas import tpu as pltpu

EPS = 1e-5
CAUSAL_CHUNK = 64
GATE_CHUNK = 128
TOP_K = 2

V7X_SUBLANES = 8
V7X_SCOPED_VMEM_BYTES = 60000 * 1024

MIX_ROWS = 256
CONV_ROW_BLOCK = 32
FFN_ROWS = 512
FFN_COLS = 512
MOE_SUB = 256
MOE_SUBS_PER_VISIT = 9
MOE_SUB_UNROLL = 3
MOE_COLS = 256
COMBINE_ROWS = 128
ROW_DMA_UNROLL = 8

F32 = jnp.float32
BF16 = jnp.bfloat16


def _round_up(n, m):
    return -(-n // m) * m


def _vmem_limit(estimate_bytes):
    return int(min(V7X_SCOPED_VMEM_BYTES, estimate_bytes + (8 << 20)))


def _rmsnorm(x, g):
    return x * lax.rsqrt(jnp.mean(x * x, axis=-1, keepdims=True) + EPS) * g


def _layernorm(x, g, b):
    mu = jnp.mean(x, axis=-1, keepdims=True)
    xc = x - mu
    return xc * lax.rsqrt(jnp.mean(xc * xc, axis=-1, keepdims=True) + EPS) * g + b


def _dot_bf16x3(a, b):
    a_hi = a.astype(BF16)
    a_lo = (a - a_hi.astype(F32)).astype(BF16)
    b_hi = b.astype(BF16)
    b_lo = (b - b_hi.astype(F32)).astype(BF16)
    dot = functools.partial(jnp.dot, preferred_element_type=F32)
    return dot(a_hi, b_hi) + (dot(a_hi, b_lo) + dot(a_lo, b_hi))


def _resident(shape):
    nd = len(shape)
    return pl.BlockSpec(shape, lambda *_: (0,) * nd, pipeline_mode=pl.Buffered(1))


def _mixer_tiling(n_seq, seq_len):
    seg_len = min(seq_len, MIX_ROWS)
    n_seg = 1 if seq_len >= MIX_ROWS else n_seq
    tiles_per_seq = seq_len // seg_len if n_seg == 1 else 1
    assert seq_len % seg_len == 0 and (seg_len * n_seg) % (2 * V7X_SUBLANES) == 0
    return seg_len, n_seg, tiles_per_seq, n_seq // n_seg


def _conv_geometry(ksize, seg_len):
    pad = _round_up(ksize - 1, V7X_SUBLANES)
    off = pad - (ksize - 1)
    residues = sorted({(off + k) % V7X_SUBLANES for k in range(ksize)} - {0})
    return pad, off, residues, pad + seg_len - V7X_SUBLANES


def _causal_conv_segments(src, state0_ref, ext_ref, shift_ref, dst_ref, state_out_ref, cw_ref,
                          bias, *, seg_len, n_seg, first_tile):
    ksize = cw_ref.shape[0]
    width = src.shape[-1]
    pad, off, residues, n_shift = _conv_geometry(ksize, seg_len)
    for s in range(n_seg):
        @pl.when(first_tile)
        def _():
            ext_ref[off:pad, :] = state0_ref[s]

        ext_ref[pad:pad + seg_len, :] = src[s * seg_len:(s + 1) * seg_len]
        for i, r in enumerate(residues):
            shift_ref[i] = ext_ref[r:r + n_shift, :]
        for r0 in range(0, seg_len, CONV_ROW_BLOCK):
            rb = min(CONV_ROW_BLOCK, seg_len - r0)
            acc = jnp.broadcast_to(bias, (rb, width))
            for k in range(ksize):
                q, r = divmod(off + k, V7X_SUBLANES)
                lo = q * V7X_SUBLANES + r0
                if r == 0:
                    tap = ext_ref[lo:lo + rb, :]
                else:
                    tap = shift_ref[residues.index(r), lo:lo + rb, :]
                acc = acc + cw_ref[k:k + 1, :] * tap
            dst_ref[s * seg_len + r0:s * seg_len + r0 + rb, :] = acc
        state_out_ref[s] = ext_ref[off + seg_len:pad + seg_len, :]
        if n_seg == 1:
            ext_ref[0:pad, :] = ext_ref[seg_len:seg_len + pad, :]


def _l0_mixer_kernel(x_ref, state0_ref, g_ref, win_ref, bin_ref, cw_ref, cb_ref,
                     lag_ref, lab_ref, lvg_ref, lvb_ref, ws_ref, bs_ref, wout_ref,
                     x1_ref, state_ref, v_ref, ext_ref, shift_ref, y_ref, ab_ref,
                     *, seg_len, n_seg):
    c = cb_ref.shape[-1]
    heads, lc, _ = ws_ref.shape
    hd = c // heads
    rows = seg_len * n_seg

    x = x_ref[...]
    h = _rmsnorm(x, g_ref[...]).astype(BF16)
    z = jnp.dot(h, win_ref[...], preferred_element_type=F32) + bin_ref[...]
    a = z[:, :c] * jax.nn.sigmoid(z[:, c:2 * c])
    u = z[:, 2 * c:3 * c]
    v = z[:, 3 * c:]

    _causal_conv_segments(a, state0_ref, ext_ref, shift_ref, y_ref, state_ref, cw_ref,
                          cb_ref[...], seg_len=seg_len, n_seg=n_seg,
                          first_tile=pl.program_id(1) == 0)
    a_act = _layernorm(y_ref[...], lag_ref[...], lab_ref[...])
    ab_ref[:, :c] = (a_act * jax.nn.sigmoid(a_act)).astype(BF16)

    vn = _layernorm(v, lvg_ref[...], lvb_ref[...])
    v_ref[...] = vn
    vb = vn.astype(BF16)
    for ci in range(rows // lc):
        r = slice(ci * lc, (ci + 1) * lc)
        for hh in range(heads):
            cs = slice(hh * hd, (hh + 1) * hd)
            s = jnp.dot(ws_ref[hh], vb[r, cs], preferred_element_type=F32) + bs_ref[:, cs]
            ab_ref[r, c + hh * hd:c + (hh + 1) * hd] = (u[r, cs] * s).astype(BF16)

    x1_ref[...] = x + jnp.dot(ab_ref[...], wout_ref[...], preferred_element_type=F32)


def _l0_mixer(x, state0, weights, *, n_seq, seq_len, keep_v):
    g, win, b_in, cw, cb, lag, lab, lvg, lvb, ws, bs, wout = weights
    m, d = x.shape
    c = cb.shape[-1]
    ksize = cw.shape[0]
    seg_len, n_seg, tiles_per_seq, groups = _mixer_tiling(n_seq, seq_len)
    rows = seg_len * n_seg
    assert rows % GATE_CHUNK == 0
    pad, _, residues, n_shift = _conv_geometry(ksize, seg_len)

    row_map = lambda i, t: (i * tiles_per_seq + t, 0)
    state_spec = pl.BlockSpec((n_seg, ksize - 1, c), lambda i, t: (i, 0, 0))
    in_specs = [pl.BlockSpec((rows, d), row_map), state_spec] + [_resident(w.shape) for w in weights]
    out_shape = (
        jax.ShapeDtypeStruct((m, d), F32),
        jax.ShapeDtypeStruct((n_seq, ksize - 1, c), F32),
        jax.ShapeDtypeStruct((m if keep_v else rows, c), F32),
    )
    out_specs = (
        pl.BlockSpec((rows, d), row_map),
        state_spec,
        pl.BlockSpec((rows, c), row_map if keep_v else (lambda i, t: (0, 0))),
    )
    est = (2 * (win.size + wout.size) + 4 * rows * d * 4 + 4 * rows * 4 * c * 4
           + 4 * rows * c * 4 + (1 + len(residues)) * (pad + seg_len) * c * 4)
    return pl.pallas_call(
        functools.partial(_l0_mixer_kernel, seg_len=seg_len, n_seg=n_seg),
        grid=(groups, tiles_per_seq),
        in_specs=in_specs,
        out_specs=out_specs,
        out_shape=out_shape,
        scratch_shapes=[
            pltpu.VMEM((pad + seg_len, c), F32),
            pltpu.VMEM((len(residues), n_shift, c), F32),
            pltpu.VMEM((rows, c), F32),
            pltpu.VMEM((rows, 2 * c), BF16),
        ],
        compiler_params=pltpu.CompilerParams(
            dimension_semantics=("arbitrary", "arbitrary"), vmem_limit_bytes=_vmem_limit(est)),
        name="l0_mixer",
    )(x, state0, *weights)


def _ffn_kernel(x_ref, g_ref, wg_ref, wu_ref, wd_ref, o_ref, xn_ref, acc_ref):
    f = pl.program_id(1)

    @pl.when(f == 0)
    def _():
        xn_ref[...] = _rmsnorm(x_ref[...], g_ref[...]).astype(BF16)
        acc_ref[...] = jnp.zeros_like(acc_ref)

    xn = xn_ref[...]
    gate = jnp.dot(xn, wg_ref[...], preferred_element_type=F32)
    up = jnp.dot(xn, wu_ref[...], preferred_element_type=F32)
    hid = (gate * jax.nn.sigmoid(gate) * up).astype(BF16)
    acc_ref[...] += jnp.dot(hid, wd_ref[...], preferred_element_type=F32)

    @pl.when(f == pl.num_programs(1) - 1)
    def _():
        o_ref[...] = x_ref[...] + acc_ref[...]


def _ffn(x, g, wg, wu, wd):
    m, d = x.shape
    hidden = wg.shape[1]
    rows = min(FFN_ROWS, m)
    cols = min(FFN_COLS, hidden)
    assert m % rows == 0 and hidden % cols == 0 and rows % (2 * V7X_SUBLANES) == 0
    est = 4 * rows * d * 4 + rows * d * 4 + rows * d * 2 + 2 * 3 * d * cols * 2 + 3 * rows * cols * 4
    return pl.pallas_call(
        _ffn_kernel,
        grid=(m // rows, hidden // cols),
        in_specs=[
            pl.BlockSpec((rows, d), lambda i, f: (i, 0)),
            pl.BlockSpec((1, d), lambda i, f: (0, 0)),
            pl.BlockSpec((d, cols), lambda i, f: (0, f)),
            pl.BlockSpec((d, cols), lambda i, f: (0, f)),
            pl.BlockSpec((cols, d), lambda i, f: (f, 0)),
        ],
        out_specs=pl.BlockSpec((rows, d), lambda i, f: (i, 0)),
        out_shape=jax.ShapeDtypeStruct((m, d), F32),
        scratch_shapes=[pltpu.VMEM((rows, d), BF16), pltpu.VMEM((rows, d), F32)],
        compiler_params=pltpu.CompilerParams(
            dimension_semantics=("arbitrary", "arbitrary"), vmem_limit_bytes=_vmem_limit(est)),
        name="l0_ffn",
    )(x, g, wg, wu, wd)


def _top2_gates(logits):
    n_exp = logits.shape[-1]
    lane = lax.broadcasted_iota(jnp.int32, logits.shape, 1)
    m1 = jnp.max(logits, axis=-1, keepdims=True)
    i1 = jnp.min(jnp.where(logits == m1, lane, n_exp), axis=-1, keepdims=True)
    sel1 = lane == i1
    rest = jnp.where(sel1, -jnp.inf, logits)
    m2 = jnp.max(rest, axis=-1, keepdims=True)
    i2 = jnp.min(jnp.where(rest == m2, lane, n_exp), axis=-1, keepdims=True)
    sel2 = lane == i2
    e2 = jnp.exp(m2 - m1)
    denom = 1.0 + e2
    gates = jnp.where(sel1, 1.0 / denom, 0.0) + jnp.where(sel2, e2 / denom, 0.0)
    return gates, (sel1 | sel2).astype(jnp.int32)


def _l1_mixer_kernel(x_ref, state0_ref, tail_ref, g_ref, win_ref, cw_ref, wout_ref, gf_ref, rt_ref,
                     x3_ref, xn_ref, gates_ref, sel_ref, state_ref, ext_ref, shift_ref, y_ref,
                     *, seg_len, n_seg, n_groups, tail_rows):
    cw = cw_ref.shape[-1]

    def mixer_tile():
        x = x_ref[...]
        h = _rmsnorm(x, g_ref[...]).astype(BF16)
        z = jnp.dot(h, win_ref[...], preferred_element_type=F32)
        b_g = z[:, :cw]
        p = z[:, cw:2 * cw] * z[:, 2 * cw:]
        _causal_conv_segments(p, state0_ref, ext_ref, shift_ref, y_ref, state_ref, cw_ref,
                              jnp.zeros((1, cw), F32),
                              seg_len=seg_len, n_seg=n_seg, first_tile=pl.program_id(1) == 0)
        q = (b_g * y_ref[...]).astype(BF16)
        x3 = x + jnp.dot(q, wout_ref[...], preferred_element_type=F32)
        x3_ref[...] = x3
        xn = _rmsnorm(x3, gf_ref[...])
        xn_ref[...] = xn
        gates, sel = _top2_gates(_dot_bf16x3(xn, rt_ref[...]))
        gates_ref[...] = gates
        sel_ref[...] = sel

    if tail_rows == 0:
        mixer_tile()
    else:
        pl.when(pl.program_id(0) < n_groups)(mixer_tile)

        @pl.when(jnp.logical_and(pl.program_id(0) == n_groups, pl.program_id(1) == 0))
        def _():
            xn_ref[0:tail_rows, :] = tail_ref[...]


def _l1_mixer(x, state0, xn_tail, weights, *, n_seq, seq_len, tail_rows):
    g, win, cw, wout, gf, router = weights
    m, d = x.shape
    width = cw.shape[-1]
    ksize = cw.shape[0]
    n_exp = router.shape[-1]
    seg_len, n_seg, tiles_per_seq, groups = _mixer_tiling(n_seq, seq_len)
    rows = seg_len * n_seg
    n_blocks = m // rows
    assert tail_rows in (0, xn_tail.shape[0]) and tail_rows <= rows
    pad, _, residues, n_shift = _conv_geometry(ksize, seg_len)

    def row_map(last):
        return lambda i, t: (jnp.minimum(i * tiles_per_seq + t, last), 0)

    state_spec = pl.BlockSpec((n_seg, ksize - 1, width),
                              lambda i, t: (jnp.minimum(i, groups - 1), 0, 0))
    body_rows = pl.BlockSpec((rows, d), row_map(n_blocks - 1))
    in_specs = ([body_rows, state_spec, _resident(xn_tail.shape)]
                + [_resident(w.shape) for w in weights])
    out_shape = (
        jax.ShapeDtypeStruct((m, d), F32),
        jax.ShapeDtypeStruct((m + tail_rows, d), F32),
        jax.ShapeDtypeStruct((m, n_exp), F32),
        jax.ShapeDtypeStruct((m, n_exp), jnp.int32),
        jax.ShapeDtypeStruct((n_seq, ksize - 1, width), F32),
    )
    out_specs = (
        body_rows,
        pl.BlockSpec((rows, d), row_map(n_blocks if tail_rows else n_blocks - 1)),
        pl.BlockSpec((rows, n_exp), row_map(n_blocks - 1)),
        pl.BlockSpec((rows, n_exp), row_map(n_blocks - 1)),
        state_spec,
    )
    est = (2 * (win.size + wout.size) + 6 * rows * d * 4 + 3 * rows * 3 * width * 4
           + 3 * rows * width * 4 + xn_tail.size * 4)
    return pl.pallas_call(
        functools.partial(_l1_mixer_kernel, seg_len=seg_len, n_seg=n_seg, n_groups=groups,
                          tail_rows=tail_rows),
        grid=(groups + (1 if tail_rows else 0), tiles_per_seq),
        in_specs=in_specs,
        out_specs=out_specs,
        out_shape=out_shape,
        scratch_shapes=[
            pltpu.VMEM((pad + seg_len, width), F32),
            pltpu.VMEM((len(residues), n_shift, width), F32),
            pltpu.VMEM((rows, width), F32),
        ],
        compiler_params=pltpu.CompilerParams(
            dimension_semantics=("arbitrary", "arbitrary"), vmem_limit_bytes=_vmem_limit(est)),
        name="l1_mixer",
    )(x, state0, xn_tail, *weights)


def _next_tile_table(table):
    tiles, n = table.shape
    return jnp.stack([table, jnp.roll(table, -1, axis=0)], axis=1).reshape(2 * tiles, 1, n)


def _gather_rows_kernel(src_ref, x_hbm, o_ref, buf, sems, *, rows):
    i = pl.program_id(0)
    n = pl.num_programs(0)
    slot = i % 2

    def issue(tile_slot, table_row):
        def body(j, carry):
            pltpu.make_async_copy(x_hbm.at[pl.ds(src_ref[table_row, 0, j], 1)],
                                  buf.at[tile_slot, pl.ds(j, 1)], sems.at[tile_slot]).start()
            return carry

        lax.fori_loop(0, rows, body, 0, unroll=ROW_DMA_UNROLL)

    @pl.when(i == 0)
    def _():
        issue(0, 0)

    @pl.when(i + 1 < n)
    def _():
        issue(1 - slot, 1)

    pltpu.make_async_copy(buf.at[slot], buf.at[slot], sems.at[slot]).wait()
    o_ref[...] = buf[slot].astype(o_ref.dtype)


def _gather_rows(x, src, *, rows, out_dtype):
    n_out = src.shape[0]
    width = x.shape[1]
    assert n_out % rows == 0
    tiles = n_out // rows
    return pl.pallas_call(
        functools.partial(_gather_rows_kernel, rows=rows),
        grid=(tiles,),
        in_specs=[
            pl.BlockSpec((2, 1, rows), lambda i: (i, 0, 0), memory_space=pltpu.SMEM),
            pl.BlockSpec(memory_space=pl.ANY),
        ],
        out_specs=pl.BlockSpec((rows, width), lambda i: (i, 0)),
        out_shape=jax.ShapeDtypeStruct((n_out, width), out_dtype),
        scratch_shapes=[pltpu.VMEM((2, rows, width), x.dtype), pltpu.SemaphoreType.DMA((2,))],
        compiler_params=pltpu.CompilerParams(dimension_semantics=("arbitrary",)),
        name="moe_gather",
    )(_next_tile_table(src.reshape(tiles, rows)), x)


def _moe_kernel(ve_ref, vrow_ref, vsub_ref, tail_ref, xs_hbm, wg_ref, wu_ref, wd_ref,
                y_hbm, xbuf, acc, wgb, wub, wdb, sem_in, sem_out):
    del ve_ref
    v = pl.program_id(0)
    f = pl.program_id(1)
    n_sub = vsub_ref[v]
    row0 = vrow_ref[v]

    def sub_rows(s):
        if isinstance(s, int):
            return pl.ds(s * MOE_SUB, MOE_SUB)
        return pl.ds(pl.multiple_of(s * MOE_SUB, MOE_SUB), MOE_SUB)

    def slab_copies(src_of, dst_of, sem, count):
        def each(method):
            def body(s, carry):
                getattr(pltpu.make_async_copy(src_of(s), dst_of(s), sem), method)()
                return carry
            return lambda: lax.fori_loop(0, count, body, 0)
        return each("start"), each("wait")

    def buf_rows(ref):
        return lambda s: ref.at[sub_rows(s)]

    def hbm_rows(ref, base):
        return lambda s: ref.at[pl.ds(pl.multiple_of(base + s * MOE_SUB, MOE_SUB), MOE_SUB)]

    def zero_acc(s, carry):
        acc[sub_rows(s), :] = jnp.zeros((MOE_SUB, acc.shape[1]), F32)
        return carry

    @pl.when(jnp.logical_and(v == 0, f == 0))
    def _():
        zero_acc(0, 0)
        start, wait = slab_copies(lambda s: acc.at[sub_rows(0)], hbm_rows(y_hbm, tail_ref[0]),
                                  sem_out, tail_ref[1])
        start()
        wait()

    @pl.when(n_sub > 0)
    def _():
        @pl.when(f == 0)
        def _():
            start, wait = slab_copies(hbm_rows(xs_hbm, row0), buf_rows(xbuf), sem_in, n_sub)
            start()
            lax.fori_loop(0, n_sub, zero_acc, 0)
            wait()

        def sub_tile(s, wg, wu, wd):
            r = sub_rows(s)
            xt = xbuf[r, :]
            gate = jnp.dot(xt, wg, preferred_element_type=F32)
            up = jnp.dot(xt, wu, preferred_element_type=F32)
            hid = (gate * jax.nn.sigmoid(gate) * up).astype(BF16)
            acc[r, :] += jnp.dot(hid, wd, preferred_element_type=F32)

        @pl.when(n_sub == MOE_SUBS_PER_VISIT)
        def _():
            wg = wg_ref[...].astype(BF16)
            wu = wu_ref[...].astype(BF16)
            wd = wd_ref[...].astype(BF16)
            for s in range(MOE_SUBS_PER_VISIT):
                sub_tile(s, wg, wu, wd)

        @pl.when(n_sub < MOE_SUBS_PER_VISIT)
        def _():
            wgb[...] = wg_ref[...].astype(BF16)
            wub[...] = wu_ref[...].astype(BF16)
            wdb[...] = wd_ref[...].astype(BF16)

            def group(i, carry):
                for j in range(MOE_SUB_UNROLL):
                    sub_tile(i * MOE_SUB_UNROLL + j, wgb[...], wub[...], wdb[...])
                return carry

            def single(s, carry):
                sub_tile(s, wgb[...], wub[...], wdb[...])
                return carry

            n_group = n_sub // MOE_SUB_UNROLL
            lax.fori_loop(0, n_group, group, 0)
            lax.fori_loop(n_group * MOE_SUB_UNROLL, n_sub, single, 0)

        @pl.when(f == pl.num_programs(1) - 1)
        def _():
            start, wait = slab_copies(buf_rows(acc), hbm_rows(y_hbm, row0), sem_out, n_sub)
            start()
            wait()


def _moe_experts(xs, visit_expert, visit_row, visit_subs, tail, wg, wu, wd):
    p_rows = xs.shape[0]
    n_visits = visit_expert.shape[0]
    _, d, hidden = wg.shape
    assert xs.shape[1] == d and xs.dtype == BF16
    cols = min(MOE_COLS, hidden)
    assert hidden % cols == 0
    n_f = hidden // cols
    slab = MOE_SUBS_PER_VISIT * MOE_SUB

    def col_tile(v, f, vs):
        return jnp.where(vs[v] > 0, f, n_f - 1)

    est = slab * d * (2 + 4) + 2 * 3 * d * cols * 4 + 3 * d * cols * 2 + 4 * MOE_SUB * d * 4
    grid_spec = pltpu.PrefetchScalarGridSpec(
        num_scalar_prefetch=4,
        grid=(n_visits, n_f),
        in_specs=[
            pl.BlockSpec(memory_space=pl.ANY),
            pl.BlockSpec((None, d, cols), lambda v, f, ve, vr, vs, tl: (ve[v], 0, col_tile(v, f, vs))),
            pl.BlockSpec((None, d, cols), lambda v, f, ve, vr, vs, tl: (ve[v], 0, col_tile(v, f, vs))),
            pl.BlockSpec((None, cols, d), lambda v, f, ve, vr, vs, tl: (ve[v], col_tile(v, f, vs), 0)),
        ],
        out_specs=pl.BlockSpec(memory_space=pl.ANY),
        scratch_shapes=[
            pltpu.VMEM((slab, d), BF16),
            pltpu.VMEM((slab, d), F32),
            pltpu.VMEM((d, cols), BF16),
            pltpu.VMEM((d, cols), BF16),
            pltpu.VMEM((cols, d), BF16),
            pltpu.SemaphoreType.DMA(()),
            pltpu.SemaphoreType.DMA(()),
        ],
    )
    return pl.pallas_call(
        _moe_kernel,
        grid_spec=grid_spec,
        out_shape=jax.ShapeDtypeStruct((p_rows, d), F32),
        compiler_params=pltpu.CompilerParams(
            dimension_semantics=("arbitrary", "arbitrary"), vmem_limit_bytes=_vmem_limit(est)),
        name="moe_experts",
    )(visit_expert, visit_row, visit_subs, tail, xs, wg, wu, wd)


def _combine_kernel(pos_ref, xp_ref, xs_ref, gate_ref, g_ref, y_hbm, op_ref, os_ref, ybuf, sems,
                    *, rows, prompt_tiles):
    i = pl.program_id(0)
    n = pl.num_programs(0)
    slot = i % 2

    def issue(tile_slot, pos_row):
        def body(j, carry):
            for k in range(TOP_K):
                pltpu.make_async_copy(y_hbm.at[pl.ds(pos_ref[pos_row, 0, TOP_K * j + k], 1)],
                                      ybuf.at[tile_slot, k, pl.ds(j, 1)],
                                      sems.at[tile_slot]).start()
            return carry

        lax.fori_loop(0, rows, body, 0, unroll=ROW_DMA_UNROLL)

    @pl.when(i == 0)
    def _():
        issue(0, 0)

    @pl.when(i + 1 < n)
    def _():
        issue(1 - slot, 1)

    pltpu.make_async_copy(ybuf.at[slot], ybuf.at[slot], sems.at[slot]).wait()
    gate = gate_ref[...]
    x = jnp.where(i < prompt_tiles, xp_ref[...], xs_ref[...])
    out = x + gate[:, 0:1] * ybuf[slot, 0] + gate[:, 1:2] * ybuf[slot, 1]
    out = _rmsnorm(out, g_ref[...])

    @pl.when(i < prompt_tiles)
    def _():
        op_ref[...] = out

    @pl.when(i >= prompt_tiles)
    def _():
        os_ref[...] = out


def _combine(x3_p, x3_s, y_sorted, pos_tok, gate_tok, g_final):
    m_p, d = x3_p.shape
    m_s = x3_s.shape[0]
    rows = COMBINE_ROWS
    assert m_p % rows == 0 and m_s == rows
    prompt_tiles = m_p // rows
    tiles = prompt_tiles + 1
    pos_pair = _next_tile_table(pos_tok.reshape(tiles, rows * TOP_K))
    est = 2 * TOP_K * rows * d * 4 + 8 * rows * d * 4
    return pl.pallas_call(
        functools.partial(_combine_kernel, rows=rows, prompt_tiles=prompt_tiles),
        grid=(tiles,),
        in_specs=[
            pl.BlockSpec((2, 1, rows * TOP_K), lambda i: (i, 0, 0), memory_space=pltpu.SMEM),
            pl.BlockSpec((rows, d), lambda i: (jnp.minimum(i, prompt_tiles - 1), 0)),
            pl.BlockSpec((rows, d), lambda i: (0, 0)),
            pl.BlockSpec((rows, TOP_K), lambda i: (i, 0)),
            pl.BlockSpec((1, d), lambda i: (0, 0)),
            pl.BlockSpec(memory_space=pl.ANY),
        ],
        out_specs=(
            pl.BlockSpec((rows, d), lambda i: (jnp.minimum(i, prompt_tiles - 1), 0)),
            pl.BlockSpec((rows, d), lambda i: (0, 0)),
        ),
        out_shape=(jax.ShapeDtypeStruct((m_p, d), F32), jax.ShapeDtypeStruct((m_s, d), F32)),
        scratch_shapes=[pltpu.VMEM((2, TOP_K, rows, d), F32), pltpu.SemaphoreType.DMA((2,))],
        compiler_params=pltpu.CompilerParams(
            dimension_semantics=("arbitrary",), vmem_limit_bytes=_vmem_limit(est)),
        name="moe_combine",
    )(pos_pair, x3_p, x3_s, gate_tok, g_final, y_sorted)


def _routing_tables(sel, gates):
    m, n_exp = sel.shape
    p_rows = _round_up(m * TOP_K, MOE_SUB) + n_exp * MOE_SUB
    counts = jnp.sum(sel, axis=0)
    subs = (counts + MOE_SUB - 1) // MOE_SUB
    group_rows = subs * MOE_SUB
    group_start = jnp.cumsum(group_rows) - group_rows
    rank = jnp.cumsum(sel, axis=0) - sel
    pos_full = group_start[None, :] + rank
    lane = jnp.arange(n_exp, dtype=jnp.int32)[None, :]
    first = jnp.argmax(sel, axis=1).astype(jnp.int32)
    second = jnp.argmax(jnp.where(lane == first[:, None], 0, sel), axis=1).astype(jnp.int32)
    order = jnp.stack([first, second], axis=1)
    pos_tok = jnp.take_along_axis(pos_full, order, axis=1).astype(jnp.int32)
    gate_tok = jnp.take_along_axis(gates, order, axis=1)
    token = jnp.broadcast_to(jnp.arange(m, dtype=jnp.int32)[:, None], (m, TOP_K))
    src = jnp.zeros((p_rows,), jnp.int32).at[pos_tok.reshape(-1)].set(token.reshape(-1))

    n_visits = n_exp + (p_rows // MOE_SUB) // MOE_SUBS_PER_VISIT
    visits_per = (subs + MOE_SUBS_PER_VISIT - 1) // MOE_SUBS_PER_VISIT
    visit_end = jnp.cumsum(visits_per)
    vid = jnp.arange(n_visits, dtype=jnp.int32)
    valid = vid < visit_end[-1]
    owner = jnp.minimum(vid, visit_end[-1] - 1)
    expert = jnp.sum(visit_end[None, :] <= owner[:, None], axis=1).astype(jnp.int32)
    j = vid - (visit_end - visits_per)[expert]
    visit_subs = jnp.where(valid, jnp.clip(subs[expert] - j * MOE_SUBS_PER_VISIT, 0, MOE_SUBS_PER_VISIT), 0)
    visit_row = jnp.where(valid, group_start[expert] + j * MOE_SUBS_PER_VISIT * MOE_SUB, 0)
    total = jnp.sum(group_rows)
    tail = jnp.stack([total, (p_rows - total) // MOE_SUB]).astype(jnp.int32)
    return (src, pos_tok, gate_tok, expert, visit_row.astype(jnp.int32),
            visit_subs.astype(jnp.int32), tail)


def _gate_tables(w_s, b_s, seq_len, width):
    heads = w_s.shape[0]
    length = min(seq_len, GATE_CHUNK)
    pos = jnp.arange(length)
    mask = (pos[:, None] // CAUSAL_CHUNK) >= (pos[None, :] // CAUSAL_CHUNK)
    w = jnp.where(mask[None], w_s[:, :length, :length], 0.0)
    reps = GATE_CHUNK // length
    if reps > 1:
        w = jnp.einsum("ab,hts->hatbs", jnp.eye(reps, dtype=w.dtype), w).reshape(
            heads, GATE_CHUNK, GATE_CHUNK)
    bias = jnp.tile(b_s[:, :length].T, (reps, 1))
    bias = jnp.repeat(bias, width // heads, axis=1)
    return w.astype(BF16), bias


def kernel(x_prompt, x_sample, cache_conv_a, cache_conv_c, l0_norm_mix, l0_w_in, l0_b_in, l0_conv_w, l0_conv_b, l0_ln_a_g, l0_ln_a_b, l0_ln_v_g, l0_ln_v_b, l0_w_s, l0_b_s, l0_w_out, l0_norm_ffn, l0_ffn_gate, l0_ffn_up, l0_ffn_down, l1_norm_mix, l1_w_in, l1_conv_w, l1_w_out, l1_norm_ffn, l1_router, l1_moe_gate, l1_moe_up, l1_moe_down, final_norm):
    n_p, t_p, d = x_prompt.shape
    n_s, t_s, _ = x_sample.shape
    m_p, m_s = n_p * t_p, n_s * t_s
    a_width = l0_conv_b.shape[0]
    row = lambda vec: vec.reshape(1, -1)
    xp = x_prompt.reshape(m_p, d)
    xs = x_sample.reshape(m_s, d)

    l0_head = (row(l0_norm_mix), l0_w_in.astype(BF16), row(l0_b_in), l0_conv_w, row(l0_conv_b),
               row(l0_ln_a_g), row(l0_ln_a_b), row(l0_ln_v_g), row(l0_ln_v_b))
    wout0 = l0_w_out.astype(BF16)
    zero_a = jnp.zeros((n_p,) + cache_conv_a.shape[1:], F32)
    xp, conv_a_prompt, _ = _l0_mixer(
        xp, zero_a, l0_head + _gate_tables(l0_w_s, l0_b_s, t_p, a_width) + (wout0,),
        n_seq=n_p, seq_len=t_p, keep_v=False)
    xs, conv_a_sample, v_sample = _l0_mixer(
        xs, cache_conv_a, l0_head + _gate_tables(l0_w_s, l0_b_s, t_s, a_width) + (wout0,),
        n_seq=n_s, seq_len=t_s, keep_v=True)

    ffn_w = (row(l0_norm_ffn), l0_ffn_gate.astype(BF16), l0_ffn_up.astype(BF16),
             l0_ffn_down.astype(BF16))
    xp = _ffn(xp, *ffn_w)
    xs = _ffn(xs, *ffn_w)

    l1_w = (row(l1_norm_mix), l1_w_in.astype(BF16), l1_conv_w, l1_w_out.astype(BF16),
            row(l1_norm_ffn), l1_router)
    zero_c = jnp.zeros((n_p,) + cache_conv_c.shape[1:], F32)
    no_tail = jnp.zeros((V7X_SUBLANES, d), F32)
    xs, xn_s, gates_s, sel_s, conv_c_sample = _l1_mixer(
        xs, cache_conv_c, no_tail, l1_w, n_seq=n_s, seq_len=t_s, tail_rows=0)
    xp, xn, gates_p, sel_p, conv_c_prompt = _l1_mixer(
        xp, zero_c, xn_s, l1_w, n_seq=n_p, seq_len=t_p, tail_rows=m_s)

    sel = jnp.concatenate([sel_p, sel_s], axis=0)
    gates = jnp.concatenate([gates_p, gates_s], axis=0)
    src, pos_tok, gate_tok, v_expert, v_row, v_subs, tail = _routing_tables(sel, gates)
    x_sorted = _gather_rows(xn, src, rows=MOE_SUB, out_dtype=BF16)
    y_sorted = _moe_experts(x_sorted, v_expert, v_row, v_subs, tail,
                            l1_moe_gate, l1_moe_up, l1_moe_down)
    y_p, y_s = _combine(xp, xs, y_sorted, pos_tok, gate_tok, row(final_norm))

    return (y_p.reshape(n_p, t_p, d), y_s.reshape(n_s, t_s, d), conv_a_prompt, conv_a_sample,
            v_sample.reshape(n_s, t_s, a_width), conv_c_prompt, conv_c_sample)
```

```python
import functools

import jax
import jax.numpy as jnp
from jax import lax
from jax.experimental import pallas as pl
from jax.experimental.pallas import tpu as pltpu

EPS = 1e-5
CAUSAL_CHUNK = 64
GATE_CHUNK = 128
TOP_K = 2

V7X_SUBLANES = 8
V7X_SCOPED_VMEM_BYTES = 60000 * 1024

MIX_ROWS = 256
CONV_ROW_BLOCK = 32
FFN_ROWS = 512
FFN_COLS = 512
MOE_UNIT = 128
MOE_SUB = 2 * MOE_UNIT
MOE_UNITS_PER_VISIT = 18
MOE_STRAIGHT_LINE_UNITS = (MOE_UNITS_PER_VISIT - 1, MOE_UNITS_PER_VISIT)
MOE_SUB_UNROLL = 3
MOE_COLS = 256
COMBINE_ROWS = 128
ROW_DMA_UNROLL = 8

F32 = jnp.float32
BF16 = jnp.bfloat16


def _round_up(n, m):
    return -(-n // m) * m


def _vmem_limit(estimate_bytes):
    return int(min(V7X_SCOPED_VMEM_BYTES, estimate_bytes + (8 << 20)))


def _rmsnorm(x, g):
    return x * lax.rsqrt(jnp.mean(x * x, axis=-1, keepdims=True) + EPS) * g


def _layernorm(x, g, b):
    mu = jnp.mean(x, axis=-1, keepdims=True)
    xc = x - mu
    return xc * lax.rsqrt(jnp.mean(xc * xc, axis=-1, keepdims=True) + EPS) * g + b


def _dot_bf16x3(a, b):
    a_hi = a.astype(BF16)
    a_lo = (a - a_hi.astype(F32)).astype(BF16)
    b_hi = b.astype(BF16)
    b_lo = (b - b_hi.astype(F32)).astype(BF16)
    dot = functools.partial(jnp.dot, preferred_element_type=F32)
    return dot(a_hi, b_hi) + (dot(a_hi, b_lo) + dot(a_lo, b_hi))


def _resident(shape):
    nd = len(shape)
    return pl.BlockSpec(shape, lambda *_: (0,) * nd, pipeline_mode=pl.Buffered(1))


def _mixer_tiling(n_seq, seq_len):
    seg_len = min(seq_len, MIX_ROWS)
    n_seg = 1 if seq_len >= MIX_ROWS else n_seq
    tiles_per_seq = seq_len // seg_len if n_seg == 1 else 1
    assert seq_len % seg_len == 0 and (seg_len * n_seg) % (2 * V7X_SUBLANES) == 0
    return seg_len, n_seg, tiles_per_seq, n_seq // n_seg


def _conv_geometry(ksize, seg_len):
    pad = _round_up(ksize - 1, V7X_SUBLANES)
    off = pad - (ksize - 1)
    residues = sorted({(off + k) % V7X_SUBLANES for k in range(ksize)} - {0})
    return pad, off, residues, pad + seg_len - V7X_SUBLANES


def _causal_conv_segments(src, state0_ref, ext_ref, shift_ref, dst_ref, state_out_ref, cw_ref,
                          bias, *, seg_len, n_seg, first_tile):
    ksize = cw_ref.shape[0]
    width = src.shape[-1]
    pad, off, residues, n_shift = _conv_geometry(ksize, seg_len)
    for s in range(n_seg):
        @pl.when(first_tile)
        def _():
            ext_ref[off:pad, :] = state0_ref[s]

        ext_ref[pad:pad + seg_len, :] = src[s * seg_len:(s + 1) * seg_len]
        for i, r in enumerate(residues):
            shift_ref[i] = ext_ref[r:r + n_shift, :]
        for r0 in range(0, seg_len, CONV_ROW_BLOCK):
            rb = min(CONV_ROW_BLOCK, seg_len - r0)
            acc = jnp.broadcast_to(bias, (rb, width))
            for k in range(ksize):
                q, r = divmod(off + k, V7X_SUBLANES)
                lo = q * V7X_SUBLANES + r0
                if r == 0:
                    tap = ext_ref[lo:lo + rb, :]
                else:
                    tap = shift_ref[residues.index(r), lo:lo + rb, :]
                acc = acc + cw_ref[k:k + 1, :] * tap
            dst_ref[s * seg_len + r0:s * seg_len + r0 + rb, :] = acc
        state_out_ref[s] = ext_ref[off + seg_len:pad + seg_len, :]
        if n_seg == 1:
            ext_ref[0:pad, :] = ext_ref[seg_len:seg_len + pad, :]


def _l0_mixer_kernel(x_ref, state0_ref, g_ref, win_ref, bin_ref, cw_ref, cb_ref,
                     lag_ref, lab_ref, lvg_ref, lvb_ref, ws_ref, bs_ref, wout_ref,
                     x1_ref, state_ref, v_ref, ext_ref, shift_ref, y_ref, ab_ref,
                     *, seg_len, n_seg):
    c = cb_ref.shape[-1]
    heads, lc, _ = ws_ref.shape
    hd = c // heads
    rows = seg_len * n_seg

    x = x_ref[...]
    h = _rmsnorm(x, g_ref[...]).astype(BF16)
    z = jnp.dot(h, win_ref[...], preferred_element_type=F32) + bin_ref[...]
    a = z[:, :c] * jax.nn.sigmoid(z[:, c:2 * c])
    u = z[:, 2 * c:3 * c]
    v = z[:, 3 * c:]

    _causal_conv_segments(a, state0_ref, ext_ref, shift_ref, y_ref, state_ref, cw_ref,
                          cb_ref[...], seg_len=seg_len, n_seg=n_seg,
                          first_tile=pl.program_id(1) == 0)
    a_act = _layernorm(y_ref[...], lag_ref[...], lab_ref[...])
    ab_ref[:, :c] = (a_act * jax.nn.sigmoid(a_act)).astype(BF16)

    vn = _layernorm(v, lvg_ref[...], lvb_ref[...])
    v_ref[...] = vn
    vb = vn.astype(BF16)
    for ci in range(rows // lc):
        r = slice(ci * lc, (ci + 1) * lc)
        for hh in range(heads):
            cs = slice(hh * hd, (hh + 1) * hd)
            s = jnp.dot(ws_ref[hh], vb[r, cs], preferred_element_type=F32) + bs_ref[:, cs]
            ab_ref[r, c + hh * hd:c + (hh + 1) * hd] = (u[r, cs] * s).astype(BF16)

    x1_ref[...] = x + jnp.dot(ab_ref[...], wout_ref[...], preferred_element_type=F32)


def _l0_mixer(x, state0, weights, *, n_seq, seq_len, keep_v):
    g, win, b_in, cw, cb, lag, lab, lvg, lvb, ws, bs, wout = weights
    m, d = x.shape
    c = cb.shape[-1]
    ksize = cw.shape[0]
    seg_len, n_seg, tiles_per_seq, groups = _mixer_tiling(n_seq, seq_len)
    rows = seg_len * n_seg
    assert rows % GATE_CHUNK == 0
    pad, _, residues, n_shift = _conv_geometry(ksize, seg_len)

    row_map = lambda i, t: (i * tiles_per_seq + t, 0)
    state_spec = pl.BlockSpec((n_seg, ksize - 1, c), lambda i, t: (i, 0, 0))
    in_specs = [pl.BlockSpec((rows, d), row_map), state_spec] + [_resident(w.shape) for w in weights]
    out_shape = (
        jax.ShapeDtypeStruct((m, d), F32),
        jax.ShapeDtypeStruct((n_seq, ksize - 1, c), F32),
        jax.ShapeDtypeStruct((m if keep_v else rows, c), F32),
    )
    out_specs = (
        pl.BlockSpec((rows, d), row_map),
        state_spec,
        pl.BlockSpec((rows, c), row_map if keep_v else (lambda i, t: (0, 0))),
    )
    est = (2 * (win.size + wout.size) + 4 * rows * d * 4 + 4 * rows * 4 * c * 4
           + 4 * rows * c * 4 + (1 + len(residues)) * (pad + seg_len) * c * 4)
    return pl.pallas_call(
        functools.partial(_l0_mixer_kernel, seg_len=seg_len, n_seg=n_seg),
        grid=(groups, tiles_per_seq),
        in_specs=in_specs,
        out_specs=out_specs,
        out_shape=out_shape,
        scratch_shapes=[
            pltpu.VMEM((pad + seg_len, c), F32),
            pltpu.VMEM((len(residues), n_shift, c), F32),
            pltpu.VMEM((rows, c), F32),
            pltpu.VMEM((rows, 2 * c), BF16),
        ],
        compiler_params=pltpu.CompilerParams(
            dimension_semantics=("arbitrary", "arbitrary"), vmem_limit_bytes=_vmem_limit(est)),
        name="l0_mixer",
    )(x, state0, *weights)


def _ffn_kernel(x_ref, g_ref, wg_ref, wu_ref, wd_ref, o_ref, xn_ref, acc_ref):
    f = pl.program_id(1)

    @pl.when(f == 0)
    def _():
        xn_ref[...] = _rmsnorm(x_ref[...], g_ref[...]).astype(BF16)
        acc_ref[...] = jnp.zeros_like(acc_ref)

    xn = xn_ref[...]
    gate = jnp.dot(xn, wg_ref[...], preferred_element_type=F32)
    up = jnp.dot(xn, wu_ref[...], preferred_element_type=F32)
    hid = (gate * jax.nn.sigmoid(gate) * up).astype(BF16)
    acc_ref[...] += jnp.dot(hid, wd_ref[...], preferred_element_type=F32)

    @pl.when(f == pl.num_programs(1) - 1)
    def _():
        o_ref[...] = x_ref[...] + acc_ref[...]


def _ffn(x, g, wg, wu, wd):
    m, d = x.shape
    hidden = wg.shape[1]
    rows = min(FFN_ROWS, m)
    cols = min(FFN_COLS, hidden)
    assert m % rows == 0 and hidden % cols == 0 and rows % (2 * V7X_SUBLANES) == 0
    est = 4 * rows * d * 4 + rows * d * 4 + rows * d * 2 + 2 * 3 * d * cols * 2 + 3 * rows * cols * 4
    return pl.pallas_call(
        _ffn_kernel,
        grid=(m // rows, hidden // cols),
        in_specs=[
            pl.BlockSpec((rows, d), lambda i, f: (i, 0)),
            pl.BlockSpec((1, d), lambda i, f: (0, 0)),
            pl.BlockSpec((d, cols), lambda i, f: (0, f)),
            pl.BlockSpec((d, cols), lambda i, f: (0, f)),
            pl.BlockSpec((cols, d), lambda i, f: (f, 0)),
        ],
        out_specs=pl.BlockSpec((rows, d), lambda i, f: (i, 0)),
        out_shape=jax.ShapeDtypeStruct((m, d), F32),
        scratch_shapes=[pltpu.VMEM((rows, d), BF16), pltpu.VMEM((rows, d), F32)],
        compiler_params=pltpu.CompilerParams(
            dimension_semantics=("arbitrary", "arbitrary"), vmem_limit_bytes=_vmem_limit(est)),
        name="l0_ffn",
    )(x, g, wg, wu, wd)


def _top2_gates(logits):
    n_exp = logits.shape[-1]
    lane = lax.broadcasted_iota(jnp.int32, logits.shape, 1)
    m1 = jnp.max(logits, axis=-1, keepdims=True)
    i1 = jnp.min(jnp.where(logits == m1, lane, n_exp), axis=-1, keepdims=True)
    sel1 = lane == i1
    rest = jnp.where(sel1, -jnp.inf, logits)
    m2 = jnp.max(rest, axis=-1, keepdims=True)
    i2 = jnp.min(jnp.where(rest == m2, lane, n_exp), axis=-1, keepdims=True)
    sel2 = lane == i2
    e2 = jnp.exp(m2 - m1)
    denom = 1.0 + e2
    gates = jnp.where(sel1, 1.0 / denom, 0.0) + jnp.where(sel2, e2 / denom, 0.0)
    return gates, (sel1 | sel2).astype(jnp.int32)


def _l1_mixer_kernel(x_ref, state0_ref, tail_ref, g_ref, win_ref, cw_ref, wout_ref, gf_ref, rt_ref,
                     x3_ref, xn_ref, gates_ref, sel_ref, state_ref, ext_ref, shift_ref, y_ref,
                     *, seg_len, n_seg, n_groups, tail_rows):
    cw = cw_ref.shape[-1]

    def mixer_tile():
        x = x_ref[...]
        h = _rmsnorm(x, g_ref[...]).astype(BF16)
        z = jnp.dot(h, win_ref[...], preferred_element_type=F32)
        b_g = z[:, :cw]
        p = z[:, cw:2 * cw] * z[:, 2 * cw:]
        _causal_conv_segments(p, state0_ref, ext_ref, shift_ref, y_ref, state_ref, cw_ref,
                              jnp.zeros((1, cw), F32),
                              seg_len=seg_len, n_seg=n_seg, first_tile=pl.program_id(1) == 0)
        q = (b_g * y_ref[...]).astype(BF16)
        x3 = x + jnp.dot(q, wout_ref[...], preferred_element_type=F32)
        x3_ref[...] = x3
        xn = _rmsnorm(x3, gf_ref[...])
        xn_ref[...] = xn
        gates, sel = _top2_gates(_dot_bf16x3(xn, rt_ref[...]))
        gates_ref[...] = gates
        sel_ref[...] = sel

    if tail_rows == 0:
        mixer_tile()
    else:
        pl.when(pl.program_id(0) < n_groups)(mixer_tile)

        @pl.when(jnp.logical_and(pl.program_id(0) == n_groups, pl.program_id(1) == 0))
        def _():
            xn_ref[0:tail_rows, :] = tail_ref[...]


def _l1_mixer(x, state0, xn_tail, weights, *, n_seq, seq_len, tail_rows):
    g, win, cw, wout, gf, router = weights
    m, d = x.shape
    width = cw.shape[-1]
    ksize = cw.shape[0]
    n_exp = router.shape[-1]
    seg_len, n_seg, tiles_per_seq, groups = _mixer_tiling(n_seq, seq_len)
    rows = seg_len * n_seg
    n_blocks = m // rows
    assert tail_rows in (0, xn_tail.shape[0]) and tail_rows <= rows
    pad, _, residues, n_shift = _conv_geometry(ksize, seg_len)

    def row_map(last):
        return lambda i, t: (jnp.minimum(i * tiles_per_seq + t, last), 0)

    state_spec = pl.BlockSpec((n_seg, ksize - 1, width),
                              lambda i, t: (jnp.minimum(i, groups - 1), 0, 0))
    body_rows = pl.BlockSpec((rows, d), row_map(n_blocks - 1))
    in_specs = ([body_rows, state_spec, _resident(xn_tail.shape)]
                + [_resident(w.shape) for w in weights])
    out_shape = (
        jax.ShapeDtypeStruct((m, d), F32),
        jax.ShapeDtypeStruct((m + tail_rows, d), F32),
        jax.ShapeDtypeStruct((m, n_exp), F32),
        jax.ShapeDtypeStruct((m, n_exp), jnp.int32),
        jax.ShapeDtypeStruct((n_seq, ksize - 1, width), F32),
    )
    out_specs = (
        body_rows,
        pl.BlockSpec((rows, d), row_map(n_blocks if tail_rows else n_blocks - 1)),
        pl.BlockSpec((rows, n_exp), row_map(n_blocks - 1)),
        pl.BlockSpec((rows, n_exp), row_map(n_blocks - 1)),
        state_spec,
    )
    est = (2 * (win.size + wout.size) + 6 * rows * d * 4 + 3 * rows * 3 * width * 4
           + 3 * rows * width * 4 + xn_tail.size * 4)
    return pl.pallas_call(
        functools.partial(_l1_mixer_kernel, seg_len=seg_len, n_seg=n_seg, n_groups=groups,
                          tail_rows=tail_rows),
        grid=(groups + (1 if tail_rows else 0), tiles_per_seq),
        in_specs=in_specs,
        out_specs=out_specs,
        out_shape=out_shape,
        scratch_shapes=[
            pltpu.VMEM((pad + seg_len, width), F32),
            pltpu.VMEM((len(residues), n_shift, width), F32),
            pltpu.VMEM((rows, width), F32),
        ],
        compiler_params=pltpu.CompilerParams(
            dimension_semantics=("arbitrary", "arbitrary"), vmem_limit_bytes=_vmem_limit(est)),
        name="l1_mixer",
    )(x, state0, xn_tail, *weights)


def _next_tile_table(table):
    tiles, n = table.shape
    return jnp.stack([table, jnp.roll(table, -1, axis=0)], axis=1).reshape(2 * tiles, 1, n)


def _gather_rows_kernel(src_ref, x_hbm, o_ref, buf, sems, *, rows):
    i = pl.program_id(0)
    n = pl.num_programs(0)
    slot = i % 2

    def issue(tile_slot, table_row):
        def body(j, carry):
            pltpu.make_async_copy(x_hbm.at[pl.ds(src_ref[table_row, 0, j], 1)],
                                  buf.at[tile_slot, pl.ds(j, 1)], sems.at[tile_slot]).start()
            return carry

        lax.fori_loop(0, rows, body, 0, unroll=ROW_DMA_UNROLL)

    @pl.when(i == 0)
    def _():
        issue(0, 0)

    @pl.when(i + 1 < n)
    def _():
        issue(1 - slot, 1)

    pltpu.make_async_copy(buf.at[slot], buf.at[slot], sems.at[slot]).wait()
    o_ref[...] = buf[slot].astype(o_ref.dtype)


def _gather_rows(x, src, *, rows, out_dtype):
    n_out = src.shape[0]
    width = x.shape[1]
    assert n_out % rows == 0
    tiles = n_out // rows
    return pl.pallas_call(
        functools.partial(_gather_rows_kernel, rows=rows),
        grid=(tiles,),
        in_specs=[
            pl.BlockSpec((2, 1, rows), lambda i: (i, 0, 0), memory_space=pltpu.SMEM),
            pl.BlockSpec(memory_space=pl.ANY),
        ],
        out_specs=pl.BlockSpec((rows, width), lambda i: (i, 0)),
        out_shape=jax.ShapeDtypeStruct((n_out, width), out_dtype),
        scratch_shapes=[pltpu.VMEM((2, rows, width), x.dtype), pltpu.SemaphoreType.DMA((2,))],
        compiler_params=pltpu.CompilerParams(dimension_semantics=("arbitrary",)),
        name="moe_gather",
    )(_next_tile_table(src.reshape(tiles, rows)), x)


def _moe_kernel(ve_ref, vrow_ref, vunit_ref, tail_ref, xs_hbm, wg_ref, wu_ref, wd_ref,
                y_hbm, xbuf, acc, wgb, wub, wdb, sem_in, sem_out):
    del ve_ref
    v = pl.program_id(0)
    f = pl.program_id(1)
    n_unit = vunit_ref[v]
    row0 = vrow_ref[v]

    def rows_at(start, n_rows):
        if isinstance(start, int):
            return pl.ds(start, n_rows)
        return pl.ds(pl.multiple_of(start, MOE_UNIT), n_rows)

    def unit_copies(src_of, dst_of, sem, count):
        def each(method):
            def body(u, carry):
                getattr(pltpu.make_async_copy(src_of(u), dst_of(u), sem), method)()
                return carry
            return lambda: lax.fori_loop(0, count, body, 0)
        return each("start"), each("wait")

    def buf_unit(ref):
        return lambda u: ref.at[rows_at(u * MOE_UNIT, MOE_UNIT)]

    def hbm_unit(ref, base):
        return lambda u: ref.at[rows_at(base + u * MOE_UNIT, MOE_UNIT)]

    def zero_unit(u, carry):
        acc[rows_at(u * MOE_UNIT, MOE_UNIT), :] = jnp.zeros((MOE_UNIT, acc.shape[1]), F32)
        return carry

    @pl.when(jnp.logical_and(v == 0, f == 0))
    def _():
        zero_unit(0, 0)
        start, wait = unit_copies(lambda u: acc.at[rows_at(0, MOE_UNIT)],
                                  hbm_unit(y_hbm, tail_ref[0]), sem_out, tail_ref[1])
        start()
        wait()

    @pl.when(n_unit > 0)
    def _():
        @pl.when(f == 0)
        def _():
            start, wait = unit_copies(hbm_unit(xs_hbm, row0), buf_unit(xbuf), sem_in, n_unit)
            start()
            lax.fori_loop(0, n_unit, zero_unit, 0)
            wait()

        def sub_tile(start, n_rows, wg, wu, wd):
            r = rows_at(start, n_rows)
            xt = xbuf[r, :]
            gate = jnp.dot(xt, wg, preferred_element_type=F32)
            up = jnp.dot(xt, wu, preferred_element_type=F32)
            hid = (gate * jax.nn.sigmoid(gate) * up).astype(BF16)
            acc[r, :] += jnp.dot(hid, wd, preferred_element_type=F32)

        def straight_line(units):
            wg = wg_ref[...].astype(BF16)
            wu = wu_ref[...].astype(BF16)
            wd = wd_ref[...].astype(BF16)
            for s in range(units // 2):
                sub_tile(s * MOE_SUB, MOE_SUB, wg, wu, wd)
            if units % 2:
                sub_tile((units // 2) * MOE_SUB, MOE_UNIT, wg, wu, wd)

        for units in MOE_STRAIGHT_LINE_UNITS:
            pl.when(n_unit == units)(functools.partial(straight_line, units))

        @pl.when(n_unit < min(MOE_STRAIGHT_LINE_UNITS))
        def _():
            wgb[...] = wg_ref[...].astype(BF16)
            wub[...] = wu_ref[...].astype(BF16)
            wdb[...] = wd_ref[...].astype(BF16)
            n_sub = n_unit // 2

            def group(i, carry):
                for j in range(MOE_SUB_UNROLL):
                    sub_tile((i * MOE_SUB_UNROLL + j) * MOE_SUB, MOE_SUB, wgb[...], wub[...], wdb[...])
                return carry

            def single(s, carry):
                sub_tile(s * MOE_SUB, MOE_SUB, wgb[...], wub[...], wdb[...])
                return carry

            n_group = n_sub // MOE_SUB_UNROLL
            lax.fori_loop(0, n_group, group, 0)
            lax.fori_loop(n_group * MOE_SUB_UNROLL, n_sub, single, 0)

            @pl.when(n_unit % 2 == 1)
            def _():
                sub_tile(n_sub * MOE_SUB, MOE_UNIT, wgb[...], wub[...], wdb[...])

        @pl.when(f == pl.num_programs(1) - 1)
        def _():
            start, wait = unit_copies(buf_unit(acc), hbm_unit(y_hbm, row0), sem_out, n_unit)
            start()
            wait()


def _moe_experts(xs, visit_expert, visit_row, visit_units, tail, wg, wu, wd):
    p_rows = xs.shape[0]
    n_visits = visit_expert.shape[0]
    _, d, hidden = wg.shape
    assert xs.shape[1] == d and xs.dtype == BF16
    cols = min(MOE_COLS, hidden)
    assert hidden % cols == 0
    n_f = hidden // cols
    slab = MOE_UNITS_PER_VISIT * MOE_UNIT

    def col_tile(v, f, vs):
        return jnp.where(vs[v] > 0, f, n_f - 1)

    est = slab * d * (2 + 4) + 2 * 3 * d * cols * 4 + 3 * d * cols * 2 + 4 * MOE_SUB * d * 4
    grid_spec = pltpu.PrefetchScalarGridSpec(
        num_scalar_prefetch=4,
        grid=(n_visits, n_f),
        in_specs=[
            pl.BlockSpec(memory_space=pl.ANY),
            pl.BlockSpec((None, d, cols), lambda v, f, ve, vr, vs, tl: (ve[v], 0, col_tile(v, f, vs))),
            pl.BlockSpec((None, d, cols), lambda v, f, ve, vr, vs, tl: (ve[v], 0, col_tile(v, f, vs))),
            pl.BlockSpec((None, cols, d), lambda v, f, ve, vr, vs, tl: (ve[v], col_tile(v, f, vs), 0)),
        ],
        out_specs=pl.BlockSpec(memory_space=pl.ANY),
        scratch_shapes=[
            pltpu.VMEM((slab, d), BF16),
            pltpu.VMEM((slab, d), F32),
            pltpu.VMEM((d, cols), BF16),
            pltpu.VMEM((d, cols), BF16),
            pltpu.VMEM((cols, d), BF16),
            pltpu.SemaphoreType.DMA(()),
            pltpu.SemaphoreType.DMA(()),
        ],
    )
    return pl.pallas_call(
        _moe_kernel,
        grid_spec=grid_spec,
        out_shape=jax.ShapeDtypeStruct((p_rows, d), F32),
        compiler_params=pltpu.CompilerParams(
            dimension_semantics=("arbitrary", "arbitrary"), vmem_limit_bytes=_vmem_limit(est)),
        name="moe_experts",
    )(visit_expert, visit_row, visit_units, tail, xs, wg, wu, wd)


def _combine_kernel(pos_ref, xp_ref, xs_ref, gate_ref, g_ref, y_hbm, op_ref, os_ref, ybuf, sems,
                    *, rows, prompt_tiles):
    i = pl.program_id(0)
    n = pl.num_programs(0)
    slot = i % 2

    def issue(tile_slot, pos_row):
        def body(j, carry):
            for k in range(TOP_K):
                pltpu.make_async_copy(y_hbm.at[pl.ds(pos_ref[pos_row, 0, TOP_K * j + k], 1)],
                                      ybuf.at[tile_slot, k, pl.ds(j, 1)],
                                      sems.at[tile_slot]).start()
            return carry

        lax.fori_loop(0, rows, body, 0, unroll=ROW_DMA_UNROLL)

    @pl.when(i == 0)
    def _():
        issue(0, 0)

    @pl.when(i + 1 < n)
    def _():
        issue(1 - slot, 1)

    pltpu.make_async_copy(ybuf.at[slot], ybuf.at[slot], sems.at[slot]).wait()
    gate = gate_ref[...]
    x = jnp.where(i < prompt_tiles, xp_ref[...], xs_ref[...])
    out = x + gate[:, 0:1] * ybuf[slot, 0] + gate[:, 1:2] * ybuf[slot, 1]
    out = _rmsnorm(out, g_ref[...])

    @pl.when(i < prompt_tiles)
    def _():
        op_ref[...] = out

    @pl.when(i >= prompt_tiles)
    def _():
        os_ref[...] = out


def _combine(x3_p, x3_s, y_sorted, pos_tok, gate_tok, g_final):
    m_p, d = x3_p.shape
    m_s = x3_s.shape[0]
    rows = COMBINE_ROWS
    assert m_p % rows == 0 and m_s == rows
    prompt_tiles = m_p // rows
    tiles = prompt_tiles + 1
    pos_pair = _next_tile_table(pos_tok.reshape(tiles, rows * TOP_K))
    est = 2 * TOP_K * rows * d * 4 + 8 * rows * d * 4
    return pl.pallas_call(
        functools.partial(_combine_kernel, rows=rows, prompt_tiles=prompt_tiles),
        grid=(tiles,),
        in_specs=[
            pl.BlockSpec((2, 1, rows * TOP_K), lambda i: (i, 0, 0), memory_space=pltpu.SMEM),
            pl.BlockSpec((rows, d), lambda i: (jnp.minimum(i, prompt_tiles - 1), 0)),
            pl.BlockSpec((rows, d), lambda i: (0, 0)),
            pl.BlockSpec((rows, TOP_K), lambda i: (i, 0)),
            pl.BlockSpec((1, d), lambda i: (0, 0)),
            pl.BlockSpec(memory_space=pl.ANY),
        ],
        out_specs=(
            pl.BlockSpec((rows, d), lambda i: (jnp.minimum(i, prompt_tiles - 1), 0)),
            pl.BlockSpec((rows, d), lambda i: (0, 0)),
        ),
        out_shape=(jax.ShapeDtypeStruct((m_p, d), F32), jax.ShapeDtypeStruct((m_s, d), F32)),
        scratch_shapes=[pltpu.VMEM((2, TOP_K, rows, d), F32), pltpu.SemaphoreType.DMA((2,))],
        compiler_params=pltpu.CompilerParams(
            dimension_semantics=("arbitrary",), vmem_limit_bytes=_vmem_limit(est)),
        name="moe_combine",
    )(pos_pair, x3_p, x3_s, gate_tok, g_final, y_sorted)


def _routing_tables(sel, gates):
    m, n_exp = sel.shape
    p_rows = _round_up(m * TOP_K + n_exp * MOE_UNIT, MOE_SUB)
    counts = jnp.sum(sel, axis=0)
    units = (counts + MOE_UNIT - 1) // MOE_UNIT
    group_rows = units * MOE_UNIT
    group_start = jnp.cumsum(group_rows) - group_rows
    rank = jnp.cumsum(sel, axis=0) - sel
    pos_full = group_start[None, :] + rank
    lane = jnp.arange(n_exp, dtype=jnp.int32)[None, :]
    first = jnp.argmax(sel, axis=1).astype(jnp.int32)
    second = jnp.argmax(jnp.where(lane == first[:, None], 0, sel), axis=1).astype(jnp.int32)
    order = jnp.stack([first, second], axis=1)
    pos_tok = jnp.take_along_axis(pos_full, order, axis=1).astype(jnp.int32)
    gate_tok = jnp.take_along_axis(gates, order, axis=1)
    token = jnp.broadcast_to(jnp.arange(m, dtype=jnp.int32)[:, None], (m, TOP_K))
    src = jnp.zeros((p_rows,), jnp.int32).at[pos_tok.reshape(-1)].set(token.reshape(-1))

    n_visits = n_exp + (p_rows // MOE_UNIT) // MOE_UNITS_PER_VISIT
    visits_per = (units + MOE_UNITS_PER_VISIT - 1) // MOE_UNITS_PER_VISIT
    visit_end = jnp.cumsum(visits_per)
    vid = jnp.arange(n_visits, dtype=jnp.int32)
    valid = vid < visit_end[-1]
    owner = jnp.minimum(vid, visit_end[-1] - 1)
    expert = jnp.sum(visit_end[None, :] <= owner[:, None], axis=1).astype(jnp.int32)
    j = vid - (visit_end - visits_per)[expert]
    visit_units = jnp.where(
        valid, jnp.clip(units[expert] - j * MOE_UNITS_PER_VISIT, 0, MOE_UNITS_PER_VISIT), 0)
    visit_row = jnp.where(valid, group_start[expert] + j * MOE_UNITS_PER_VISIT * MOE_UNIT, 0)
    total = jnp.sum(group_rows)
    tail = jnp.stack([total, (p_rows - total) // MOE_UNIT]).astype(jnp.int32)
    return (src, pos_tok, gate_tok, expert, visit_row.astype(jnp.int32),
            visit_units.astype(jnp.int32), tail)


def _gate_tables(w_s, b_s, seq_len, width):
    heads = w_s.shape[0]
    length = min(seq_len, GATE_CHUNK)
    pos = jnp.arange(length)
    mask = (pos[:, None] // CAUSAL_CHUNK) >= (pos[None, :] // CAUSAL_CHUNK)
    w = jnp.where(mask[None], w_s[:, :length, :length], 0.0)
    reps = GATE_CHUNK // length
    if reps > 1:
        w = jnp.einsum("ab,hts->hatbs", jnp.eye(reps, dtype=w.dtype), w).reshape(
            heads, GATE_CHUNK, GATE_CHUNK)
    bias = jnp.tile(b_s[:, :length].T, (reps, 1))
    bias = jnp.repeat(bias, width // heads, axis=1)
    return w.astype(BF16), bias


def kernel(x_prompt, x_sample, cache_conv_a, cache_conv_c, l0_norm_mix, l0_w_in, l0_b_in, l0_conv_w, l0_conv_b, l0_ln_a_g, l0_ln_a_b, l0_ln_v_g, l0_ln_v_b, l0_w_s, l0_b_s, l0_w_out, l0_norm_ffn, l0_ffn_gate, l0_ffn_up, l0_ffn_down, l1_norm_mix, l1_w_in, l1_conv_w, l1_w_out, l1_norm_ffn, l1_router, l1_moe_gate, l1_moe_up, l1_moe_down, final_norm):
    n_p, t_p, d = x_prompt.shape
    n_s, t_s, _ = x_sample.shape
    m_p, m_s = n_p * t_p, n_s * t_s
    a_width = l0_conv_b.shape[0]
    row = lambda vec: vec.reshape(1, -1)
    xp = x_prompt.reshape(m_p, d)
    xs = x_sample.reshape(m_s, d)

    l0_head = (row(l0_norm_mix), l0_w_in.astype(BF16), row(l0_b_in), l0_conv_w, row(l0_conv_b),
               row(l0_ln_a_g), row(l0_ln_a_b), row(l0_ln_v_g), row(l0_ln_v_b))
    wout0 = l0_w_out.astype(BF16)
    zero_a = jnp.zeros((n_p,) + cache_conv_a.shape[1:], F32)
    xp, conv_a_prompt, _ = _l0_mixer(
        xp, zero_a, l0_head + _gate_tables(l0_w_s, l0_b_s, t_p, a_width) + (wout0,),
        n_seq=n_p, seq_len=t_p, keep_v=False)
    xs, conv_a_sample, v_sample = _l0_mixer(
        xs, cache_conv_a, l0_head + _gate_tables(l0_w_s, l0_b_s, t_s, a_width) + (wout0,),
        n_seq=n_s, seq_len=t_s, keep_v=True)

    ffn_w = (row(l0_norm_ffn), l0_ffn_gate.astype(BF16), l0_ffn_up.astype(BF16),
             l0_ffn_down.astype(BF16))
    xp = _ffn(xp, *ffn_w)
    xs = _ffn(xs, *ffn_w)

    l1_w = (row(l1_norm_mix), l1_w_in.astype(BF16), l1_conv_w, l1_w_out.astype(BF16),
            row(l1_norm_ffn), l1_router)
    zero_c = jnp.zeros((n_p,) + cache_conv_c.shape[1:], F32)
    no_tail = jnp.zeros((V7X_SUBLANES, d), F32)
    xs, xn_s, gates_s, sel_s, conv_c_sample = _l1_mixer(
        xs, cache_conv_c, no_tail, l1_w, n_seq=n_s, seq_len=t_s, tail_rows=0)
    xp, xn, gates_p, sel_p, conv_c_prompt = _l1_mixer(
        xp, zero_c, xn_s, l1_w, n_seq=n_p, seq_len=t_p, tail_rows=m_s)

    sel = jnp.concatenate([sel_p, sel_s], axis=0)
    gates = jnp.concatenate([gates_p, gates_s], axis=0)
    src, pos_tok, gate_tok, v_expert, v_row, v_subs, tail = _routing_tables(sel, gates)
    x_sorted = _gather_rows(xn, src, rows=MOE_SUB, out_dtype=BF16)
    y_sorted = _moe_experts(x_sorted, v_expert, v_row, v_subs, tail,
                            l1_moe_gate, l1_moe_up, l1_moe_down)
    y_p, y_s = _combine(xp, xs, y_sorted, pos_tok, gate_tok, row(final_norm))

    return (y_p.reshape(n_p, t_p, d), y_s.reshape(n_s, t_s, d), conv_a_prompt, conv_a_sample,
            v_sample.reshape(n_s, t_s, a_width), conv_c_prompt, conv_c_sample)
```

```python
import functools

import jax
import jax.numpy as jnp
from jax import lax
from jax.experimental import pallas as pl
from jax.experimental.pallas import tpu as pltpu

EPS = 1e-5
CAUSAL_CHUNK = 64
GATE_CHUNK = 128
TOP_K = 2

V7X_SUBLANES = 8
V7X_SCOPED_VMEM_BYTES = 60000 * 1024

MIX_ROWS = 256
CONV_ROW_BLOCK = 32
FFN_ROWS = 512
FFN_COLS = 512
MOE_UNIT = 128
MOE_SUB = 2 * MOE_UNIT
MOE_UNITS_PER_VISIT = 18
MOE_STRAIGHT_LINE_UNITS = (MOE_UNITS_PER_VISIT - 1, MOE_UNITS_PER_VISIT)
MOE_SUB_UNROLL = 3
MOE_COLS = 256
MOE_OUT_COLS = 256
COMBINE_ROWS = 128
ROW_DMA_UNROLL = 8

F32 = jnp.float32
BF16 = jnp.bfloat16


def _round_up(n, m):
    return -(-n // m) * m


def _vmem_limit(estimate_bytes):
    return int(min(V7X_SCOPED_VMEM_BYTES, estimate_bytes + (8 << 20)))


def _rmsnorm(x, g):
    return x * lax.rsqrt(jnp.mean(x * x, axis=-1, keepdims=True) + EPS) * g


def _layernorm(x, g, b):
    mu = jnp.mean(x, axis=-1, keepdims=True)
    xc = x - mu
    return xc * lax.rsqrt(jnp.mean(xc * xc, axis=-1, keepdims=True) + EPS) * g + b


def _dot_bf16x3(a, b):
    a_hi = a.astype(BF16)
    a_lo = (a - a_hi.astype(F32)).astype(BF16)
    b_hi = b.astype(BF16)
    b_lo = (b - b_hi.astype(F32)).astype(BF16)
    dot = functools.partial(jnp.dot, preferred_element_type=F32)
    return dot(a_hi, b_hi) + (dot(a_hi, b_lo) + dot(a_lo, b_hi))


def _resident(shape):
    nd = len(shape)
    return pl.BlockSpec(shape, lambda *_: (0,) * nd, pipeline_mode=pl.Buffered(1))


def _mixer_tiling(n_seq, seq_len):
    seg_len = min(seq_len, MIX_ROWS)
    n_seg = 1 if seq_len >= MIX_ROWS else n_seq
    tiles_per_seq = seq_len // seg_len if n_seg == 1 else 1
    assert seq_len % seg_len == 0 and (seg_len * n_seg) % (2 * V7X_SUBLANES) == 0
    return seg_len, n_seg, tiles_per_seq, n_seq // n_seg


def _conv_geometry(ksize, seg_len):
    pad = _round_up(ksize - 1, V7X_SUBLANES)
    off = pad - (ksize - 1)
    residues = sorted({(off + k) % V7X_SUBLANES for k in range(ksize)} - {0})
    return pad, off, residues, pad + seg_len - V7X_SUBLANES


def _causal_conv_segments(src, state0_ref, ext_ref, shift_ref, dst_ref, state_out_ref, cw_ref,
                          bias, *, seg_len, n_seg, first_tile):
    ksize = cw_ref.shape[0]
    width = src.shape[-1]
    pad, off, residues, n_shift = _conv_geometry(ksize, seg_len)
    for s in range(n_seg):
        @pl.when(first_tile)
        def _():
            ext_ref[off:pad, :] = state0_ref[s]

        ext_ref[pad:pad + seg_len, :] = src[s * seg_len:(s + 1) * seg_len]
        for i, r in enumerate(residues):
            shift_ref[i] = ext_ref[r:r + n_shift, :]
        for r0 in range(0, seg_len, CONV_ROW_BLOCK):
            rb = min(CONV_ROW_BLOCK, seg_len - r0)
            acc = jnp.broadcast_to(bias, (rb, width))
            for k in range(ksize):
                q, r = divmod(off + k, V7X_SUBLANES)
                lo = q * V7X_SUBLANES + r0
                if r == 0:
                    tap = ext_ref[lo:lo + rb, :]
                else:
                    tap = shift_ref[residues.index(r), lo:lo + rb, :]
                acc = acc + cw_ref[k:k + 1, :] * tap
            dst_ref[s * seg_len + r0:s * seg_len + r0 + rb, :] = acc
        state_out_ref[s] = ext_ref[off + seg_len:pad + seg_len, :]
        if n_seg == 1:
            ext_ref[0:pad, :] = ext_ref[seg_len:seg_len + pad, :]


def _l0_mixer_kernel(x_ref, state0_ref, g_ref, win_ref, bin_ref, cw_ref, cb_ref,
                     lag_ref, lab_ref, lvg_ref, lvb_ref, ws_ref, bs_ref, wout_ref,
                     x1_ref, state_ref, v_ref, ext_ref, shift_ref, y_ref, ab_ref,
                     *, seg_len, n_seg):
    c = cb_ref.shape[-1]
    heads, lc, _ = ws_ref.shape
    hd = c // heads
    rows = seg_len * n_seg

    x = x_ref[...]
    h = _rmsnorm(x, g_ref[...]).astype(BF16)
    z = jnp.dot(h, win_ref[...], preferred_element_type=F32) + bin_ref[...]
    a = z[:, :c] * jax.nn.sigmoid(z[:, c:2 * c])
    u = z[:, 2 * c:3 * c]
    v = z[:, 3 * c:]

    _causal_conv_segments(a, state0_ref, ext_ref, shift_ref, y_ref, state_ref, cw_ref,
                          cb_ref[...], seg_len=seg_len, n_seg=n_seg,
                          first_tile=pl.program_id(1) == 0)
    a_act = _layernorm(y_ref[...], lag_ref[...], lab_ref[...])
    ab_ref[:, :c] = (a_act * jax.nn.sigmoid(a_act)).astype(BF16)

    vn = _layernorm(v, lvg_ref[...], lvb_ref[...])
    v_ref[...] = vn
    vb = vn.astype(BF16)
    for ci in range(rows // lc):
        r = slice(ci * lc, (ci + 1) * lc)
        for hh in range(heads):
            cs = slice(hh * hd, (hh + 1) * hd)
            s = jnp.dot(ws_ref[hh], vb[r, cs], preferred_element_type=F32) + bs_ref[:, cs]
            ab_ref[r, c + hh * hd:c + (hh + 1) * hd] = (u[r, cs] * s).astype(BF16)

    x1_ref[...] = x + jnp.dot(ab_ref[...], wout_ref[...], preferred_element_type=F32)


def _l0_mixer(x, state0, weights, *, n_seq, seq_len, keep_v):
    g, win, b_in, cw, cb, lag, lab, lvg, lvb, ws, bs, wout = weights
    m, d = x.shape
    c = cb.shape[-1]
    ksize = cw.shape[0]
    seg_len, n_seg, tiles_per_seq, groups = _mixer_tiling(n_seq, seq_len)
    rows = seg_len * n_seg
    assert rows % GATE_CHUNK == 0
    pad, _, residues, n_shift = _conv_geometry(ksize, seg_len)

    row_map = lambda i, t: (i * tiles_per_seq + t, 0)
    state_spec = pl.BlockSpec((n_seg, ksize - 1, c), lambda i, t: (i, 0, 0))
    in_specs = [pl.BlockSpec((rows, d), row_map), state_spec] + [_resident(w.shape) for w in weights]
    out_shape = (
        jax.ShapeDtypeStruct((m, d), F32),
        jax.ShapeDtypeStruct((n_seq, ksize - 1, c), F32),
        jax.ShapeDtypeStruct((m if keep_v else rows, c), F32),
    )
    out_specs = (
        pl.BlockSpec((rows, d), row_map),
        state_spec,
        pl.BlockSpec((rows, c), row_map if keep_v else (lambda i, t: (0, 0))),
    )
    est = (2 * (win.size + wout.size) + 4 * rows * d * 4 + 4 * rows * 4 * c * 4
           + 4 * rows * c * 4 + (1 + len(residues)) * (pad + seg_len) * c * 4)
    return pl.pallas_call(
        functools.partial(_l0_mixer_kernel, seg_len=seg_len, n_seg=n_seg),
        grid=(groups, tiles_per_seq),
        in_specs=in_specs,
        out_specs=out_specs,
        out_shape=out_shape,
        scratch_shapes=[
            pltpu.VMEM((pad + seg_len, c), F32),
            pltpu.VMEM((len(residues), n_shift, c), F32),
            pltpu.VMEM((rows, c), F32),
            pltpu.VMEM((rows, 2 * c), BF16),
        ],
        compiler_params=pltpu.CompilerParams(
            dimension_semantics=("arbitrary", "arbitrary"), vmem_limit_bytes=_vmem_limit(est)),
        name="l0_mixer",
    )(x, state0, *weights)


def _ffn_kernel(x_ref, g_ref, wg_ref, wu_ref, wd_ref, o_ref, xn_ref, acc_ref):
    f = pl.program_id(1)

    @pl.when(f == 0)
    def _():
        xn_ref[...] = _rmsnorm(x_ref[...], g_ref[...]).astype(BF16)
        acc_ref[...] = jnp.zeros_like(acc_ref)

    xn = xn_ref[...]
    gate = jnp.dot(xn, wg_ref[...], preferred_element_type=F32)
    up = jnp.dot(xn, wu_ref[...], preferred_element_type=F32)
    hid = (gate * jax.nn.sigmoid(gate) * up).astype(BF16)
    acc_ref[...] += jnp.dot(hid, wd_ref[...], preferred_element_type=F32)

    @pl.when(f == pl.num_programs(1) - 1)
    def _():
        o_ref[...] = x_ref[...] + acc_ref[...]


def _ffn(x, g, wg, wu, wd):
    m, d = x.shape
    hidden = wg.shape[1]
    rows = min(FFN_ROWS, m)
    cols = min(FFN_COLS, hidden)
    assert m % rows == 0 and hidden % cols == 0 and rows % (2 * V7X_SUBLANES) == 0
    est = 4 * rows * d * 4 + rows * d * 4 + rows * d * 2 + 2 * 3 * d * cols * 2 + 3 * rows * cols * 4
    return pl.pallas_call(
        _ffn_kernel,
        grid=(m // rows, hidden // cols),
        in_specs=[
            pl.BlockSpec((rows, d), lambda i, f: (i, 0)),
            pl.BlockSpec((1, d), lambda i, f: (0, 0)),
            pl.BlockSpec((d, cols), lambda i, f: (0, f)),
            pl.BlockSpec((d, cols), lambda i, f: (0, f)),
            pl.BlockSpec((cols, d), lambda i, f: (f, 0)),
        ],
        out_specs=pl.BlockSpec((rows, d), lambda i, f: (i, 0)),
        out_shape=jax.ShapeDtypeStruct((m, d), F32),
        scratch_shapes=[pltpu.VMEM((rows, d), BF16), pltpu.VMEM((rows, d), F32)],
        compiler_params=pltpu.CompilerParams(
            dimension_semantics=("arbitrary", "arbitrary"), vmem_limit_bytes=_vmem_limit(est)),
        name="l0_ffn",
    )(x, g, wg, wu, wd)


def _top2_gates(logits):
    n_exp = logits.shape[-1]
    lane = lax.broadcasted_iota(jnp.int32, logits.shape, 1)
    m1 = jnp.max(logits, axis=-1, keepdims=True)
    i1 = jnp.min(jnp.where(logits == m1, lane, n_exp), axis=-1, keepdims=True)
    sel1 = lane == i1
    rest = jnp.where(sel1, -jnp.inf, logits)
    m2 = jnp.max(rest, axis=-1, keepdims=True)
    i2 = jnp.min(jnp.where(rest == m2, lane, n_exp), axis=-1, keepdims=True)
    sel2 = lane == i2
    e2 = jnp.exp(m2 - m1)
    denom = 1.0 + e2
    gates = jnp.where(sel1, 1.0 / denom, 0.0) + jnp.where(sel2, e2 / denom, 0.0)
    return gates, (sel1 | sel2).astype(jnp.int32)


def _l1_mixer_kernel(x_ref, state0_ref, tail_ref, g_ref, win_ref, cw_ref, wout_ref, gf_ref, rt_ref,
                     x3_ref, xn_ref, gates_ref, sel_ref, state_ref, ext_ref, shift_ref, y_ref,
                     *, seg_len, n_seg, n_groups, tail_rows):
    cw = cw_ref.shape[-1]

    def mixer_tile():
        x = x_ref[...]
        h = _rmsnorm(x, g_ref[...]).astype(BF16)
        z = jnp.dot(h, win_ref[...], preferred_element_type=F32)
        b_g = z[:, :cw]
        p = z[:, cw:2 * cw] * z[:, 2 * cw:]
        _causal_conv_segments(p, state0_ref, ext_ref, shift_ref, y_ref, state_ref, cw_ref,
                              jnp.zeros((1, cw), F32),
                              seg_len=seg_len, n_seg=n_seg, first_tile=pl.program_id(1) == 0)
        q = (b_g * y_ref[...]).astype(BF16)
        x3 = x + jnp.dot(q, wout_ref[...], preferred_element_type=F32)
        x3_ref[...] = x3
        xn = _rmsnorm(x3, gf_ref[...])
        xn_ref[...] = xn
        gates, sel = _top2_gates(_dot_bf16x3(xn, rt_ref[...]))
        gates_ref[...] = gates
        sel_ref[...] = sel

    if tail_rows == 0:
        mixer_tile()
    else:
        pl.when(pl.program_id(0) < n_groups)(mixer_tile)

        @pl.when(jnp.logical_and(pl.program_id(0) == n_groups, pl.program_id(1) == 0))
        def _():
            xn_ref[0:tail_rows, :] = tail_ref[...]


def _l1_mixer(x, state0, xn_tail, weights, *, n_seq, seq_len, tail_rows):
    g, win, cw, wout, gf, router = weights
    m, d = x.shape
    width = cw.shape[-1]
    ksize = cw.shape[0]
    n_exp = router.shape[-1]
    seg_len, n_seg, tiles_per_seq, groups = _mixer_tiling(n_seq, seq_len)
    rows = seg_len * n_seg
    n_blocks = m // rows
    assert tail_rows in (0, xn_tail.shape[0]) and tail_rows <= rows
    pad, _, residues, n_shift = _conv_geometry(ksize, seg_len)

    def row_map(last):
        return lambda i, t: (jnp.minimum(i * tiles_per_seq + t, last), 0)

    state_spec = pl.BlockSpec((n_seg, ksize - 1, width),
                              lambda i, t: (jnp.minimum(i, groups - 1), 0, 0))
    body_rows = pl.BlockSpec((rows, d), row_map(n_blocks - 1))
    in_specs = ([body_rows, state_spec, _resident(xn_tail.shape)]
                + [_resident(w.shape) for w in weights])
    out_shape = (
        jax.ShapeDtypeStruct((m, d), F32),
        jax.ShapeDtypeStruct((m + tail_rows, d), F32),
        jax.ShapeDtypeStruct((m, n_exp), F32),
        jax.ShapeDtypeStruct((m, n_exp), jnp.int32),
        jax.ShapeDtypeStruct((n_seq, ksize - 1, width), F32),
    )
    out_specs = (
        body_rows,
        pl.BlockSpec((rows, d), row_map(n_blocks if tail_rows else n_blocks - 1)),
        pl.BlockSpec((rows, n_exp), row_map(n_blocks - 1)),
        pl.BlockSpec((rows, n_exp), row_map(n_blocks - 1)),
        state_spec,
    )
    est = (2 * (win.size + wout.size) + 6 * rows * d * 4 + 3 * rows * 3 * width * 4
           + 3 * rows * width * 4 + xn_tail.size * 4)
    return pl.pallas_call(
        functools.partial(_l1_mixer_kernel, seg_len=seg_len, n_seg=n_seg, n_groups=groups,
                          tail_rows=tail_rows),
        grid=(groups + (1 if tail_rows else 0), tiles_per_seq),
        in_specs=in_specs,
        out_specs=out_specs,
        out_shape=out_shape,
        scratch_shapes=[
            pltpu.VMEM((pad + seg_len, width), F32),
            pltpu.VMEM((len(residues), n_shift, width), F32),
            pltpu.VMEM((rows, width), F32),
        ],
        compiler_params=pltpu.CompilerParams(
            dimension_semantics=("arbitrary", "arbitrary"), vmem_limit_bytes=_vmem_limit(est)),
        name="l1_mixer",
    )(x, state0, xn_tail, *weights)


def _next_tile_table(table):
    tiles, n = table.shape
    return jnp.stack([table, jnp.roll(table, -1, axis=0)], axis=1).reshape(2 * tiles, 1, n)


def _gather_rows_kernel(src_ref, x_hbm, o_ref, buf, sems, *, rows):
    i = pl.program_id(0)
    n = pl.num_programs(0)
    slot = i % 2

    def issue(tile_slot, table_row):
        def body(j, carry):
            pltpu.make_async_copy(x_hbm.at[pl.ds(src_ref[table_row, 0, j], 1)],
                                  buf.at[tile_slot, pl.ds(j, 1)], sems.at[tile_slot]).start()
            return carry

        lax.fori_loop(0, rows, body, 0, unroll=ROW_DMA_UNROLL)

    @pl.when(i == 0)
    def _():
        issue(0, 0)

    @pl.when(i + 1 < n)
    def _():
        issue(1 - slot, 1)

    pltpu.make_async_copy(buf.at[slot], buf.at[slot], sems.at[slot]).wait()
    o_ref[...] = buf[slot].astype(o_ref.dtype)


def _gather_rows(x, src, *, rows, out_dtype):
    n_out = src.shape[0]
    width = x.shape[1]
    assert n_out % rows == 0
    tiles = n_out // rows
    return pl.pallas_call(
        functools.partial(_gather_rows_kernel, rows=rows),
        grid=(tiles,),
        in_specs=[
            pl.BlockSpec((2, 1, rows), lambda i: (i, 0, 0), memory_space=pltpu.SMEM),
            pl.BlockSpec(memory_space=pl.ANY),
        ],
        out_specs=pl.BlockSpec((rows, width), lambda i: (i, 0)),
        out_shape=jax.ShapeDtypeStruct((n_out, width), out_dtype),
        scratch_shapes=[pltpu.VMEM((2, rows, width), x.dtype), pltpu.SemaphoreType.DMA((2,))],
        compiler_params=pltpu.CompilerParams(dimension_semantics=("arbitrary",)),
        name="moe_gather",
    )(_next_tile_table(src.reshape(tiles, rows)), x)


def _moe_kernel(ve_ref, vrow_ref, vunit_ref, tail_ref, xs_hbm, wg_ref, wu_ref, wd_ref,
                y_hbm, xbuf, acc, wgb, wub, wdb, sem_in, sem_out):
    del ve_ref
    v = pl.program_id(0)
    f = pl.program_id(1)
    n_unit = vunit_ref[v]
    row0 = vrow_ref[v]

    def rows_at(start, n_rows):
        if isinstance(start, int):
            return pl.ds(start, n_rows)
        return pl.ds(pl.multiple_of(start, MOE_UNIT), n_rows)

    def unit_copies(src_of, dst_of, sem, count):
        def each(method):
            def body(u, carry):
                getattr(pltpu.make_async_copy(src_of(u), dst_of(u), sem), method)()
                return carry
            return lambda: lax.fori_loop(0, count, body, 0)
        return each("start"), each("wait")

    def buf_unit(ref):
        return lambda u: ref.at[rows_at(u * MOE_UNIT, MOE_UNIT)]

    def hbm_unit(ref, base):
        return lambda u: ref.at[rows_at(base + u * MOE_UNIT, MOE_UNIT)]

    def zero_unit(u, carry):
        acc[rows_at(u * MOE_UNIT, MOE_UNIT), :] = jnp.zeros((MOE_UNIT, acc.shape[1]), F32)
        return carry

    @pl.when(jnp.logical_and(v == 0, f == 0))
    def _():
        zero_unit(0, 0)
        start, wait = unit_copies(lambda u: acc.at[rows_at(0, MOE_UNIT)],
                                  hbm_unit(y_hbm, tail_ref[0]), sem_out, tail_ref[1])
        start()
        wait()

    @pl.when(n_unit > 0)
    def _():
        @pl.when(f == 0)
        def _():
            start, wait = unit_copies(hbm_unit(xs_hbm, row0), buf_unit(xbuf), sem_in, n_unit)
            start()
            lax.fori_loop(0, n_unit, zero_unit, 0)
            wait()

        def sub_tile(start, n_rows, wg, wu, wd):
            r = rows_at(start, n_rows)
            xt = xbuf[r, :]
            gate = jnp.dot(xt, wg, preferred_element_type=F32)
            up = jnp.dot(xt, wu, preferred_element_type=F32)
            hid = (gate * jax.nn.sigmoid(gate) * up).astype(BF16)
            acc[r, :] += jnp.dot(hid, wd, preferred_element_type=F32)

        def straight_line(units):
            n_rows = units * MOE_UNIT
            wg = wg_ref[...].astype(BF16)
            wu = wu_ref[...].astype(BF16)
            wd = wd_ref[...].astype(BF16)
            xt = xbuf[0:n_rows, :]
            gate = jnp.dot(xt, wg, preferred_element_type=F32)
            up = jnp.dot(xt, wu, preferred_element_type=F32)
            hid = (gate * jax.nn.sigmoid(gate) * up).astype(BF16)
            for c0 in range(0, acc.shape[1], MOE_OUT_COLS):
                acc[0:n_rows, c0:c0 + MOE_OUT_COLS] += jnp.dot(
                    hid, wd[:, c0:c0 + MOE_OUT_COLS], preferred_element_type=F32)

        for units in MOE_STRAIGHT_LINE_UNITS:
            pl.when(n_unit == units)(functools.partial(straight_line, units))

        @pl.when(n_unit < min(MOE_STRAIGHT_LINE_UNITS))
        def _():
            wgb[...] = wg_ref[...].astype(BF16)
            wub[...] = wu_ref[...].astype(BF16)
            wdb[...] = wd_ref[...].astype(BF16)
            n_sub = n_unit // 2

            def group(i, carry):
                for j in range(MOE_SUB_UNROLL):
                    sub_tile((i * MOE_SUB_UNROLL + j) * MOE_SUB, MOE_SUB, wgb[...], wub[...], wdb[...])
                return carry

            def single(s, carry):
                sub_tile(s * MOE_SUB, MOE_SUB, wgb[...], wub[...], wdb[...])
                return carry

            n_group = n_sub // MOE_SUB_UNROLL
            lax.fori_loop(0, n_group, group, 0)
            lax.fori_loop(n_group * MOE_SUB_UNROLL, n_sub, single, 0)

            @pl.when(n_unit % 2 == 1)
            def _():
                sub_tile(n_sub * MOE_SUB, MOE_UNIT, wgb[...], wub[...], wdb[...])

        @pl.when(f == pl.num_programs(1) - 1)
        def _():
            start, wait = unit_copies(buf_unit(acc), hbm_unit(y_hbm, row0), sem_out, n_unit)
            start()
            wait()


def _moe_experts(xs, visit_expert, visit_row, visit_units, tail, wg, wu, wd):
    p_rows = xs.shape[0]
    n_visits = visit_expert.shape[0]
    _, d, hidden = wg.shape
    assert xs.shape[1] == d and xs.dtype == BF16
    cols = min(MOE_COLS, hidden)
    assert hidden % cols == 0
    n_f = hidden // cols
    slab = MOE_UNITS_PER_VISIT * MOE_UNIT

    def col_tile(v, f, vs):
        return jnp.where(vs[v] > 0, f, n_f - 1)

    est = slab * d * (2 + 4) + 2 * 3 * d * cols * 4 + 3 * d * cols * 2 + 4 * MOE_SUB * d * 4
    grid_spec = pltpu.PrefetchScalarGridSpec(
        num_scalar_prefetch=4,
        grid=(n_visits, n_f),
        in_specs=[
            pl.BlockSpec(memory_space=pl.ANY),
            pl.BlockSpec((None, d, cols), lambda v, f, ve, vr, vs, tl: (ve[v], 0, col_tile(v, f, vs))),
            pl.BlockSpec((None, d, cols), lambda v, f, ve, vr, vs, tl: (ve[v], 0, col_tile(v, f, vs))),
            pl.BlockSpec((None, cols, d), lambda v, f, ve, vr, vs, tl: (ve[v], col_tile(v, f, vs), 0)),
        ],
        out_specs=pl.BlockSpec(memory_space=pl.ANY),
        scratch_shapes=[
            pltpu.VMEM((slab, d), BF16),
            pltpu.VMEM((slab, d), F32),
            pltpu.VMEM((d, cols), BF16),
            pltpu.VMEM((d, cols), BF16),
            pltpu.VMEM((cols, d), BF16),
            pltpu.SemaphoreType.DMA(()),
            pltpu.SemaphoreType.DMA(()),
        ],
    )
    return pl.pallas_call(
        _moe_kernel,
        grid_spec=grid_spec,
        out_shape=jax.ShapeDtypeStruct((p_rows, d), F32),
        compiler_params=pltpu.CompilerParams(
            dimension_semantics=("arbitrary", "arbitrary"), vmem_limit_bytes=_vmem_limit(est)),
        name="moe_experts",
    )(visit_expert, visit_row, visit_units, tail, xs, wg, wu, wd)


def _combine_kernel(pos_ref, xp_ref, xs_ref, gate_ref, g_ref, y_hbm, op_ref, os_ref, ybuf, sems,
                    *, rows, prompt_tiles):
    i = pl.program_id(0)
    n = pl.num_programs(0)
    slot = i % 2

    def issue(tile_slot, pos_row):
        def body(j, carry):
            for k in range(TOP_K):
                pltpu.make_async_copy(y_hbm.at[pl.ds(pos_ref[pos_row, 0, TOP_K * j + k], 1)],
                                      ybuf.at[tile_slot, k, pl.ds(j, 1)],
                                      sems.at[tile_slot]).start()
            return carry

        lax.fori_loop(0, rows, body, 0, unroll=ROW_DMA_UNROLL)

    @pl.when(i == 0)
    def _():
        issue(0, 0)

    @pl.when(i + 1 < n)
    def _():
        issue(1 - slot, 1)

    pltpu.make_async_copy(ybuf.at[slot], ybuf.at[slot], sems.at[slot]).wait()
    gate = gate_ref[...]
    x = jnp.where(i < prompt_tiles, xp_ref[...], xs_ref[...])
    out = x + gate[:, 0:1] * ybuf[slot, 0] + gate[:, 1:2] * ybuf[slot, 1]
    out = _rmsnorm(out, g_ref[...])

    @pl.when(i < prompt_tiles)
    def _():
        op_ref[...] = out

    @pl.when(i >= prompt_tiles)
    def _():
        os_ref[...] = out


def _combine(x3_p, x3_s, y_sorted, pos_tok, gate_tok, g_final):
    m_p, d = x3_p.shape
    m_s = x3_s.shape[0]
    rows = COMBINE_ROWS
    assert m_p % rows == 0 and m_s == rows
    prompt_tiles = m_p // rows
    tiles = prompt_tiles + 1
    pos_pair = _next_tile_table(pos_tok.reshape(tiles, rows * TOP_K))
    est = 2 * TOP_K * rows * d * 4 + 8 * rows * d * 4
    return pl.pallas_call(
        functools.partial(_combine_kernel, rows=rows, prompt_tiles=prompt_tiles),
        grid=(tiles,),
        in_specs=[
            pl.BlockSpec((2, 1, rows * TOP_K), lambda i: (i, 0, 0), memory_space=pltpu.SMEM),
            pl.BlockSpec((rows, d), lambda i: (jnp.minimum(i, prompt_tiles - 1), 0)),
            pl.BlockSpec((rows, d), lambda i: (0, 0)),
            pl.BlockSpec((rows, TOP_K), lambda i: (i, 0)),
            pl.BlockSpec((1, d), lambda i: (0, 0)),
            pl.BlockSpec(memory_space=pl.ANY),
        ],
        out_specs=(
            pl.BlockSpec((rows, d), lambda i: (jnp.minimum(i, prompt_tiles - 1), 0)),
            pl.BlockSpec((rows, d), lambda i: (0, 0)),
        ),
        out_shape=(jax.ShapeDtypeStruct((m_p, d), F32), jax.ShapeDtypeStruct((m_s, d), F32)),
        scratch_shapes=[pltpu.VMEM((2, TOP_K, rows, d), F32), pltpu.SemaphoreType.DMA((2,))],
        compiler_params=pltpu.CompilerParams(
            dimension_semantics=("arbitrary",), vmem_limit_bytes=_vmem_limit(est)),
        name="moe_combine",
    )(pos_pair, x3_p, x3_s, gate_tok, g_final, y_sorted)


def _routing_tables(sel, gates):
    m, n_exp = sel.shape
    p_rows = _round_up(m * TOP_K + n_exp * MOE_UNIT, MOE_SUB)
    counts = jnp.sum(sel, axis=0)
    units = (counts + MOE_UNIT - 1) // MOE_UNIT
    group_rows = units * MOE_UNIT
    group_start = jnp.cumsum(group_rows) - group_rows
    rank = jnp.cumsum(sel, axis=0) - sel
    pos_full = group_start[None, :] + rank
    lane = jnp.arange(n_exp, dtype=jnp.int32)[None, :]
    first = jnp.argmax(sel, axis=1).astype(jnp.int32)
    second = jnp.argmax(jnp.where(lane == first[:, None], 0, sel), axis=1).astype(jnp.int32)
    order = jnp.stack([first, second], axis=1)
    pos_tok = jnp.take_along_axis(pos_full, order, axis=1).astype(jnp.int32)
    gate_tok = jnp.take_along_axis(gates, order, axis=1)
    token = jnp.broadcast_to(jnp.arange(m, dtype=jnp.int32)[:, None], (m, TOP_K))
    src = jnp.zeros((p_rows,), jnp.int32).at[pos_tok.reshape(-1)].set(token.reshape(-1))

    n_visits = n_exp + (p_rows // MOE_UNIT) // MOE_UNITS_PER_VISIT
    visits_per = (units + MOE_UNITS_PER_VISIT - 1) // MOE_UNITS_PER_VISIT
    visit_end = jnp.cumsum(visits_per)
    vid = jnp.arange(n_visits, dtype=jnp.int32)
    valid = vid < visit_end[-1]
    owner = jnp.minimum(vid, visit_end[-1] - 1)
    expert = jnp.sum(visit_end[None, :] <= owner[:, None], axis=1).astype(jnp.int32)
    j = vid - (visit_end - visits_per)[expert]
    visit_units = jnp.where(
        valid, jnp.clip(units[expert] - j * MOE_UNITS_PER_VISIT, 0, MOE_UNITS_PER_VISIT), 0)
    visit_row = jnp.where(valid, group_start[expert] + j * MOE_UNITS_PER_VISIT * MOE_UNIT, 0)
    total = jnp.sum(group_rows)
    tail = jnp.stack([total, (p_rows - total) // MOE_UNIT]).astype(jnp.int32)
    return (src, pos_tok, gate_tok, expert, visit_row.astype(jnp.int32),
            visit_units.astype(jnp.int32), tail)


def _gate_tables(w_s, b_s, seq_len, width):
    heads = w_s.shape[0]
    length = min(seq_len, GATE_CHUNK)
    pos = jnp.arange(length)
    mask = (pos[:, None] // CAUSAL_CHUNK) >= (pos[None, :] // CAUSAL_CHUNK)
    w = jnp.where(mask[None], w_s[:, :length, :length], 0.0)
    reps = GATE_CHUNK // length
    if reps > 1:
        w = jnp.einsum("ab,hts->hatbs", jnp.eye(reps, dtype=w.dtype), w).reshape(
            heads, GATE_CHUNK, GATE_CHUNK)
    bias = jnp.tile(b_s[:, :length].T, (reps, 1))
    bias = jnp.repeat(bias, width // heads, axis=1)
    return w.astype(BF16), bias


def kernel(x_prompt, x_sample, cache_conv_a, cache_conv_c, l0_norm_mix, l0_w_in, l0_b_in, l0_conv_w, l0_conv_b, l0_ln_a_g, l0_ln_a_b, l0_ln_v_g, l0_ln_v_b, l0_w_s, l0_b_s, l0_w_out, l0_norm_ffn, l0_ffn_gate, l0_ffn_up, l0_ffn_down, l1_norm_mix, l1_w_in, l1_conv_w, l1_w_out, l1_norm_ffn, l1_router, l1_moe_gate, l1_moe_up, l1_moe_down, final_norm):
    n_p, t_p, d = x_prompt.shape
    n_s, t_s, _ = x_sample.shape
    m_p, m_s = n_p * t_p, n_s * t_s
    a_width = l0_conv_b.shape[0]
    row = lambda vec: vec.reshape(1, -1)
    xp = x_prompt.reshape(m_p, d)
    xs = x_sample.reshape(m_s, d)

    l0_head = (row(l0_norm_mix), l0_w_in.astype(BF16), row(l0_b_in), l0_conv_w, row(l0_conv_b),
               row(l0_ln_a_g), row(l0_ln_a_b), row(l0_ln_v_g), row(l0_ln_v_b))
    wout0 = l0_w_out.astype(BF16)
    zero_a = jnp.zeros((n_p,) + cache_conv_a.shape[1:], F32)
    xp, conv_a_prompt, _ = _l0_mixer(
        xp, zero_a, l0_head + _gate_tables(l0_w_s, l0_b_s, t_p, a_width) + (wout0,),
        n_seq=n_p, seq_len=t_p, keep_v=False)
    xs, conv_a_sample, v_sample = _l0_mixer(
        xs, cache_conv_a, l0_head + _gate_tables(l0_w_s, l0_b_s, t_s, a_width) + (wout0,),
        n_seq=n_s, seq_len=t_s, keep_v=True)

    ffn_w = (row(l0_norm_ffn), l0_ffn_gate.astype(BF16), l0_ffn_up.astype(BF16),
             l0_ffn_down.astype(BF16))
    xp = _ffn(xp, *ffn_w)
    xs = _ffn(xs, *ffn_w)

    l1_w = (row(l1_norm_mix), l1_w_in.astype(BF16), l1_conv_w, l1_w_out.astype(BF16),
            row(l1_norm_ffn), l1_router)
    zero_c = jnp.zeros((n_p,) + cache_conv_c.shape[1:], F32)
    no_tail = jnp.zeros((V7X_SUBLANES, d), F32)
    xs, xn_s, gates_s, sel_s, conv_c_sample = _l1_mixer(
        xs, cache_conv_c, no_tail, l1_w, n_seq=n_s, seq_len=t_s, tail_rows=0)
    xp, xn, gates_p, sel_p, conv_c_prompt = _l1_mixer(
        xp, zero_c, xn_s, l1_w, n_seq=n_p, seq_len=t_p, tail_rows=m_s)

    sel = jnp.concatenate([sel_p, sel_s], axis=0)
    gates = jnp.concatenate([gates_p, gates_s], axis=0)
    src, pos_tok, gate_tok, v_expert, v_row, v_subs, tail = _routing_tables(sel, gates)
    x_sorted = _gather_rows(xn, src, rows=MOE_SUB, out_dtype=BF16)
    y_sorted = _moe_experts(x_sorted, v_expert, v_row, v_subs, tail,
                            l1_moe_gate, l1_moe_up, l1_moe_down)
    y_p, y_s = _combine(xp, xs, y_sorted, pos_tok, gate_tok, row(final_norm))

    return (y_p.reshape(n_p, t_p, d), y_s.reshape(n_s, t_s, d), conv_a_prompt, conv_a_sample,
            v_sample.reshape(n_s, t_s, a_width), conv_c_prompt, conv_c_sample)
```

```python
import functools

import jax
import jax.numpy as jnp
from jax import lax
from jax.experimental import pallas as pl
from jax.experimental.pallas import tpu as pltpu

EPS = 1e-5
CAUSAL_CHUNK = 64
GATE_CHUNK = 128
TOP_K = 2

V7X_SUBLANES = 8
V7X_SCOPED_VMEM_BYTES = 60000 * 1024

MIX_ROWS = 256
CONV_ROW_BLOCK = 32
FFN_ROWS = 512
FFN_COLS = 512
MOE_UNIT = 128
MOE_SUB = 2 * MOE_UNIT
MOE_UNITS_PER_VISIT = 18
MOE_STRAIGHT_LINE_UNITS = (MOE_UNITS_PER_VISIT - 1, MOE_UNITS_PER_VISIT)
MOE_SUB_UNROLL = 3
MOE_COLS = 256
MOE_OUT_COLS = 256
COMBINE_ROWS = 128
ROW_DMA_UNROLL = 8

F32 = jnp.float32
BF16 = jnp.bfloat16


def _round_up(n, m):
    return -(-n // m) * m


def _vmem_limit(estimate_bytes):
    return int(min(V7X_SCOPED_VMEM_BYTES, estimate_bytes + (8 << 20)))


def _rmsnorm(x, g):
    return x * lax.rsqrt(jnp.mean(x * x, axis=-1, keepdims=True) + EPS) * g


def _layernorm(x, g, b):
    mu = jnp.mean(x, axis=-1, keepdims=True)
    xc = x - mu
    return xc * lax.rsqrt(jnp.mean(xc * xc, axis=-1, keepdims=True) + EPS) * g + b


def _dot_bf16x3(a, b):
    a_hi = a.astype(BF16)
    a_lo = (a - a_hi.astype(F32)).astype(BF16)
    b_hi = b.astype(BF16)
    b_lo = (b - b_hi.astype(F32)).astype(BF16)
    dot = functools.partial(jnp.dot, preferred_element_type=F32)
    return dot(a_hi, b_hi) + (dot(a_hi, b_lo) + dot(a_lo, b_hi))


def _resident(shape):
    nd = len(shape)
    return pl.BlockSpec(shape, lambda *_: (0,) * nd, pipeline_mode=pl.Buffered(1))


def _mixer_tiling(n_seq, seq_len):
    seg_len = min(seq_len, MIX_ROWS)
    n_seg = 1 if seq_len >= MIX_ROWS else n_seq
    tiles_per_seq = seq_len // seg_len if n_seg == 1 else 1
    assert seq_len % seg_len == 0 and (seg_len * n_seg) % (2 * V7X_SUBLANES) == 0
    return seg_len, n_seg, tiles_per_seq, n_seq // n_seg


def _conv_geometry(ksize, seg_len):
    pad = _round_up(ksize - 1, V7X_SUBLANES)
    off = pad - (ksize - 1)
    residues = sorted({(off + k) % V7X_SUBLANES for k in range(ksize)} - {0})
    return pad, off, residues, pad + seg_len - V7X_SUBLANES


def _causal_conv_segments(src, state0_ref, ext_ref, shift_ref, dst_ref, state_out_ref, cw_ref,
                          bias, *, seg_len, n_seg, first_tile):
    ksize = cw_ref.shape[0]
    width = src.shape[-1]
    pad, off, residues, n_shift = _conv_geometry(ksize, seg_len)
    for s in range(n_seg):
        @pl.when(first_tile)
        def _():
            ext_ref[off:pad, :] = state0_ref[s]

        ext_ref[pad:pad + seg_len, :] = src[s * seg_len:(s + 1) * seg_len]
        for i, r in enumerate(residues):
            shift_ref[i] = ext_ref[r:r + n_shift, :]
        for r0 in range(0, seg_len, CONV_ROW_BLOCK):
            rb = min(CONV_ROW_BLOCK, seg_len - r0)
            acc = jnp.broadcast_to(bias, (rb, width))
            for k in range(ksize):
                q, r = divmod(off + k, V7X_SUBLANES)
                lo = q * V7X_SUBLANES + r0
                if r == 0:
                    tap = ext_ref[lo:lo + rb, :]
                else:
                    tap = shift_ref[residues.index(r), lo:lo + rb, :]
                acc = acc + cw_ref[k:k + 1, :] * tap
            dst_ref[s * seg_len + r0:s * seg_len + r0 + rb, :] = acc
        state_out_ref[s] = ext_ref[off + seg_len:pad + seg_len, :]
        if n_seg == 1:
            ext_ref[0:pad, :] = ext_ref[seg_len:seg_len + pad, :]


def _l0_mixer_kernel(x_ref, state0_ref, g_ref, win_ref, bin_ref, cw_ref, cb_ref,
                     lag_ref, lab_ref, lvg_ref, lvb_ref, ws_ref, bs_ref, wout_ref,
                     x1_ref, state_ref, v_ref, ext_ref, shift_ref, y_ref, ab_ref,
                     *, seg_len, n_seg):
    c = cb_ref.shape[-1]
    heads, lc, _ = ws_ref.shape
    hd = c // heads
    rows = seg_len * n_seg

    x = x_ref[...]
    h = _rmsnorm(x, g_ref[...]).astype(BF16)
    z = jnp.dot(h, win_ref[...], preferred_element_type=F32) + bin_ref[...]
    a = z[:, :c] * jax.nn.sigmoid(z[:, c:2 * c])
    u = z[:, 2 * c:3 * c]
    v = z[:, 3 * c:]

    _causal_conv_segments(a, state0_ref, ext_ref, shift_ref, y_ref, state_ref, cw_ref,
                          cb_ref[...], seg_len=seg_len, n_seg=n_seg,
                          first_tile=pl.program_id(1) == 0)
    a_act = _layernorm(y_ref[...], lag_ref[...], lab_ref[...])
    ab_ref[:, :c] = (a_act * jax.nn.sigmoid(a_act)).astype(BF16)

    vn = _layernorm(v, lvg_ref[...], lvb_ref[...])
    v_ref[...] = vn
    vb = vn.astype(BF16)
    for ci in range(rows // lc):
        r = slice(ci * lc, (ci + 1) * lc)
        for hh in range(heads):
            cs = slice(hh * hd, (hh + 1) * hd)
            s = jnp.dot(ws_ref[hh], vb[r, cs], preferred_element_type=F32) + bs_ref[:, cs]
            ab_ref[r, c + hh * hd:c + (hh + 1) * hd] = (u[r, cs] * s).astype(BF16)

    x1_ref[...] = x + jnp.dot(ab_ref[...], wout_ref[...], preferred_element_type=F32)


def _l0_mixer(x, state0, weights, *, n_seq, seq_len, keep_v):
    g, win, b_in, cw, cb, lag, lab, lvg, lvb, ws, bs, wout = weights
    m, d = x.shape
    c = cb.shape[-1]
    ksize = cw.shape[0]
    seg_len, n_seg, tiles_per_seq, groups = _mixer_tiling(n_seq, seq_len)
    rows = seg_len * n_seg
    assert rows % GATE_CHUNK == 0
    pad, _, residues, n_shift = _conv_geometry(ksize, seg_len)

    row_map = lambda i, t: (i * tiles_per_seq + t, 0)
    state_spec = pl.BlockSpec((n_seg, ksize - 1, c), lambda i, t: (i, 0, 0))
    in_specs = [pl.BlockSpec((rows, d), row_map), state_spec] + [_resident(w.shape) for w in weights]
    out_shape = (
        jax.ShapeDtypeStruct((m, d), F32),
        jax.ShapeDtypeStruct((n_seq, ksize - 1, c), F32),
        jax.ShapeDtypeStruct((m if keep_v else rows, c), F32),
    )
    out_specs = (
        pl.BlockSpec((rows, d), row_map),
        state_spec,
        pl.BlockSpec((rows, c), row_map if keep_v else (lambda i, t: (0, 0))),
    )
    est = (2 * (win.size + wout.size) + 4 * rows * d * 4 + 4 * rows * 4 * c * 4
           + 4 * rows * c * 4 + (1 + len(residues)) * (pad + seg_len) * c * 4)
    return pl.pallas_call(
        functools.partial(_l0_mixer_kernel, seg_len=seg_len, n_seg=n_seg),
        grid=(groups, tiles_per_seq),
        in_specs=in_specs,
        out_specs=out_specs,
        out_shape=out_shape,
        scratch_shapes=[
            pltpu.VMEM((pad + seg_len, c), F32),
            pltpu.VMEM((len(residues), n_shift, c), F32),
            pltpu.VMEM((rows, c), F32),
            pltpu.VMEM((rows, 2 * c), BF16),
        ],
        compiler_params=pltpu.CompilerParams(
            dimension_semantics=("arbitrary", "arbitrary"), vmem_limit_bytes=_vmem_limit(est)),
        name="l0_mixer",
    )(x, state0, *weights)


def _ffn_kernel(x_ref, g_ref, wg_ref, wu_ref, wd_ref, o_ref, xn_ref, acc_ref):
    f = pl.program_id(1)

    @pl.when(f == 0)
    def _():
        xn_ref[...] = _rmsnorm(x_ref[...], g_ref[...]).astype(BF16)
        acc_ref[...] = jnp.zeros_like(acc_ref)

    xn = xn_ref[...]
    gate = jnp.dot(xn, wg_ref[...], preferred_element_type=F32)
    up = jnp.dot(xn, wu_ref[...], preferred_element_type=F32)
    hid = (gate * jax.nn.sigmoid(gate) * up).astype(BF16)
    acc_ref[...] += jnp.dot(hid, wd_ref[...], preferred_element_type=F32)

    @pl.when(f == pl.num_programs(1) - 1)
    def _():
        o_ref[...] = x_ref[...] + acc_ref[...]


def _ffn(x, g, wg, wu, wd):
    m, d = x.shape
    hidden = wg.shape[1]
    rows = min(FFN_ROWS, m)
    cols = min(FFN_COLS, hidden)
    assert m % rows == 0 and hidden % cols == 0 and rows % (2 * V7X_SUBLANES) == 0
    est = 4 * rows * d * 4 + rows * d * 4 + rows * d * 2 + 2 * 3 * d * cols * 2 + 3 * rows * cols * 4
    return pl.pallas_call(
        _ffn_kernel,
        grid=(m // rows, hidden // cols),
        in_specs=[
            pl.BlockSpec((rows, d), lambda i, f: (i, 0)),
            pl.BlockSpec((1, d), lambda i, f: (0, 0)),
            pl.BlockSpec((d, cols), lambda i, f: (0, f)),
            pl.BlockSpec((d, cols), lambda i, f: (0, f)),
            pl.BlockSpec((cols, d), lambda i, f: (f, 0)),
        ],
        out_specs=pl.BlockSpec((rows, d), lambda i, f: (i, 0)),
        out_shape=jax.ShapeDtypeStruct((m, d), F32),
        scratch_shapes=[pltpu.VMEM((rows, d), BF16), pltpu.VMEM((rows, d), F32)],
        compiler_params=pltpu.CompilerParams(
            dimension_semantics=("arbitrary", "arbitrary"), vmem_limit_bytes=_vmem_limit(est)),
        name="l0_ffn",
    )(x, g, wg, wu, wd)


def _top2_gates(logits):
    n_exp = logits.shape[-1]
    lane = lax.broadcasted_iota(jnp.int32, logits.shape, 1)
    m1 = jnp.max(logits, axis=-1, keepdims=True)
    i1 = jnp.min(jnp.where(logits == m1, lane, n_exp), axis=-1, keepdims=True)
    sel1 = lane == i1
    rest = jnp.where(sel1, -jnp.inf, logits)
    m2 = jnp.max(rest, axis=-1, keepdims=True)
    i2 = jnp.min(jnp.where(rest == m2, lane, n_exp), axis=-1, keepdims=True)
    sel2 = lane == i2
    e2 = jnp.exp(m2 - m1)
    denom = 1.0 + e2
    gates = jnp.where(sel1, 1.0 / denom, 0.0) + jnp.where(sel2, e2 / denom, 0.0)
    return gates, (sel1 | sel2).astype(jnp.int32)


def _l1_mixer_kernel(x_ref, state0_ref, tail_ref, g_ref, win_ref, cw_ref, wout_ref, gf_ref, rt_ref,
                     x3_ref, xn_ref, gates_ref, sel_ref, state_ref, ext_ref, shift_ref, y_ref,
                     *, seg_len, n_seg, n_groups, tail_rows):
    cw = cw_ref.shape[-1]

    def mixer_tile():
        x = x_ref[...]
        h = _rmsnorm(x, g_ref[...]).astype(BF16)
        z = jnp.dot(h, win_ref[...], preferred_element_type=F32)
        b_g = z[:, :cw]
        p = z[:, cw:2 * cw] * z[:, 2 * cw:]
        _causal_conv_segments(p, state0_ref, ext_ref, shift_ref, y_ref, state_ref, cw_ref,
                              jnp.zeros((1, cw), F32),
                              seg_len=seg_len, n_seg=n_seg, first_tile=pl.program_id(1) == 0)
        q = (b_g * y_ref[...]).astype(BF16)
        x3 = x + jnp.dot(q, wout_ref[...], preferred_element_type=F32)
        x3_ref[...] = x3
        xn = _rmsnorm(x3, gf_ref[...])
        xn_ref[...] = xn
        gates, sel = _top2_gates(_dot_bf16x3(xn, rt_ref[...]))
        gates_ref[...] = gates
        sel_ref[...] = sel

    if tail_rows == 0:
        mixer_tile()
    else:
        pl.when(pl.program_id(0) < n_groups)(mixer_tile)

        @pl.when(jnp.logical_and(pl.program_id(0) == n_groups, pl.program_id(1) == 0))
        def _():
            xn_ref[0:tail_rows, :] = tail_ref[...]


def _l1_mixer(x, state0, xn_tail, weights, *, n_seq, seq_len, tail_rows):
    g, win, cw, wout, gf, router = weights
    m, d = x.shape
    width = cw.shape[-1]
    ksize = cw.shape[0]
    n_exp = router.shape[-1]
    seg_len, n_seg, tiles_per_seq, groups = _mixer_tiling(n_seq, seq_len)
    rows = seg_len * n_seg
    n_blocks = m // rows
    assert tail_rows in (0, xn_tail.shape[0]) and tail_rows <= rows
    pad, _, residues, n_shift = _conv_geometry(ksize, seg_len)

    def row_map(last):
        return lambda i, t: (jnp.minimum(i * tiles_per_seq + t, last), 0)

    state_spec = pl.BlockSpec((n_seg, ksize - 1, width),
                              lambda i, t: (jnp.minimum(i, groups - 1), 0, 0))
    body_rows = pl.BlockSpec((rows, d), row_map(n_blocks - 1))
    in_specs = ([body_rows, state_spec, _resident(xn_tail.shape)]
                + [_resident(w.shape) for w in weights])
    out_shape = (
        jax.ShapeDtypeStruct((m, d), F32),
        jax.ShapeDtypeStruct((m + tail_rows, d), F32),
        jax.ShapeDtypeStruct((m, n_exp), F32),
        jax.ShapeDtypeStruct((m, n_exp), jnp.int32),
        jax.ShapeDtypeStruct((n_seq, ksize - 1, width), F32),
    )
    out_specs = (
        body_rows,
        pl.BlockSpec((rows, d), row_map(n_blocks if tail_rows else n_blocks - 1)),
        pl.BlockSpec((rows, n_exp), row_map(n_blocks - 1)),
        pl.BlockSpec((rows, n_exp), row_map(n_blocks - 1)),
        state_spec,
    )
    est = (2 * (win.size + wout.size) + 6 * rows * d * 4 + 3 * rows * 3 * width * 4
           + 3 * rows * width * 4 + xn_tail.size * 4)
    return pl.pallas_call(
        functools.partial(_l1_mixer_kernel, seg_len=seg_len, n_seg=n_seg, n_groups=groups,
                          tail_rows=tail_rows),
        grid=(groups + (1 if tail_rows else 0), tiles_per_seq),
        in_specs=in_specs,
        out_specs=out_specs,
        out_shape=out_shape,
        scratch_shapes=[
            pltpu.VMEM((pad + seg_len, width), F32),
            pltpu.VMEM((len(residues), n_shift, width), F32),
            pltpu.VMEM((rows, width), F32),
        ],
        compiler_params=pltpu.CompilerParams(
            dimension_semantics=("arbitrary", "arbitrary"), vmem_limit_bytes=_vmem_limit(est)),
        name="l1_mixer",
    )(x, state0, xn_tail, *weights)


def _tile_table_specs(tiles, n):
    return [
        pl.BlockSpec((1, 1, n), lambda i: (i, 0, 0), memory_space=pltpu.SMEM),
        pl.BlockSpec((1, 1, n), lambda i: (jnp.minimum(i + 1, tiles - 1), 0, 0),
                     memory_space=pltpu.SMEM),
    ]


def _gather_rows_kernel(src_ref, src_next_ref, x_hbm, o_ref, buf, sems, *, rows):
    i = pl.program_id(0)
    n = pl.num_programs(0)
    slot = i % 2

    def issue(tile_slot, table_ref):
        def body(j, carry):
            pltpu.make_async_copy(x_hbm.at[pl.ds(table_ref[0, 0, j], 1)],
                                  buf.at[tile_slot, pl.ds(j, 1)], sems.at[tile_slot]).start()
            return carry

        lax.fori_loop(0, rows, body, 0, unroll=ROW_DMA_UNROLL)

    @pl.when(i == 0)
    def _():
        issue(0, src_ref)

    @pl.when(i + 1 < n)
    def _():
        issue(1 - slot, src_next_ref)

    pltpu.make_async_copy(buf.at[slot], buf.at[slot], sems.at[slot]).wait()
    o_ref[...] = buf[slot].astype(o_ref.dtype)


def _gather_rows(x, src, *, rows, out_dtype):
    n_out = src.shape[0]
    width = x.shape[1]
    assert n_out % rows == 0
    tiles = n_out // rows
    return pl.pallas_call(
        functools.partial(_gather_rows_kernel, rows=rows),
        grid=(tiles,),
        in_specs=_tile_table_specs(tiles, rows) + [pl.BlockSpec(memory_space=pl.ANY)],
        out_specs=pl.BlockSpec((rows, width), lambda i: (i, 0)),
        out_shape=jax.ShapeDtypeStruct((n_out, width), out_dtype),
        scratch_shapes=[pltpu.VMEM((2, rows, width), x.dtype), pltpu.SemaphoreType.DMA((2,))],
        compiler_params=pltpu.CompilerParams(dimension_semantics=("arbitrary",)),
        name="moe_gather",
    )(src.reshape(tiles, 1, rows), src.reshape(tiles, 1, rows), x)


def _moe_kernel(ve_ref, vrow_ref, vunit_ref, tail_ref, xs_hbm, wg_ref, wu_ref, wd_ref,
                y_hbm, xbuf, acc, wgb, wub, wdb, sem_in, sem_out):
    del ve_ref
    v = pl.program_id(0)
    f = pl.program_id(1)
    n_unit = vunit_ref[v]
    row0 = vrow_ref[v]

    def rows_at(start, n_rows):
        if isinstance(start, int):
            return pl.ds(start, n_rows)
        return pl.ds(pl.multiple_of(start, MOE_UNIT), n_rows)

    def unit_copies(src_of, dst_of, sem, count):
        def each(method):
            def body(u, carry):
                getattr(pltpu.make_async_copy(src_of(u), dst_of(u), sem), method)()
                return carry
            return lambda: lax.fori_loop(0, count, body, 0)
        return each("start"), each("wait")

    def buf_unit(ref):
        return lambda u: ref.at[rows_at(u * MOE_UNIT, MOE_UNIT)]

    def hbm_unit(ref, base):
        return lambda u: ref.at[rows_at(base + u * MOE_UNIT, MOE_UNIT)]

    def zero_unit(u, carry):
        acc[rows_at(u * MOE_UNIT, MOE_UNIT), :] = jnp.zeros((MOE_UNIT, acc.shape[1]), F32)
        return carry

    @pl.when(jnp.logical_and(v == 0, f == 0))
    def _():
        zero_unit(0, 0)
        start, wait = unit_copies(lambda u: acc.at[rows_at(0, MOE_UNIT)],
                                  hbm_unit(y_hbm, tail_ref[0]), sem_out, tail_ref[1])
        start()
        wait()

    @pl.when(n_unit > 0)
    def _():
        @pl.when(f == 0)
        def _():
            start, wait = unit_copies(hbm_unit(xs_hbm, row0), buf_unit(xbuf), sem_in, n_unit)
            start()
            lax.fori_loop(0, n_unit, zero_unit, 0)
            wait()

        def sub_tile(start, n_rows, wg, wu, wd):
            r = rows_at(start, n_rows)
            xt = xbuf[r, :]
            gate = jnp.dot(xt, wg, preferred_element_type=F32)
            up = jnp.dot(xt, wu, preferred_element_type=F32)
            hid = (gate * jax.nn.sigmoid(gate) * up).astype(BF16)
            acc[r, :] += jnp.dot(hid, wd, preferred_element_type=F32)

        def straight_line(units):
            n_rows = units * MOE_UNIT
            wg = wg_ref[...].astype(BF16)
            wu = wu_ref[...].astype(BF16)
            wd = wd_ref[...].astype(BF16)
            xt = xbuf[0:n_rows, :]
            gate = jnp.dot(xt, wg, preferred_element_type=F32)
            up = jnp.dot(xt, wu, preferred_element_type=F32)
            hid = (gate * jax.nn.sigmoid(gate) * up).astype(BF16)
            for c0 in range(0, acc.shape[1], MOE_OUT_COLS):
                acc[0:n_rows, c0:c0 + MOE_OUT_COLS] += jnp.dot(
                    hid, wd[:, c0:c0 + MOE_OUT_COLS], preferred_element_type=F32)

        for units in MOE_STRAIGHT_LINE_UNITS:
            pl.when(n_unit == units)(functools.partial(straight_line, units))

        @pl.when(n_unit < min(MOE_STRAIGHT_LINE_UNITS))
        def _():
            wgb[...] = wg_ref[...].astype(BF16)
            wub[...] = wu_ref[...].astype(BF16)
            wdb[...] = wd_ref[...].astype(BF16)
            n_sub = n_unit // 2

            def group(i, carry):
                for j in range(MOE_SUB_UNROLL):
                    sub_tile((i * MOE_SUB_UNROLL + j) * MOE_SUB, MOE_SUB, wgb[...], wub[...], wdb[...])
                return carry

            def single(s, carry):
                sub_tile(s * MOE_SUB, MOE_SUB, wgb[...], wub[...], wdb[...])
                return carry

            n_group = n_sub // MOE_SUB_UNROLL
            lax.fori_loop(0, n_group, group, 0)
            lax.fori_loop(n_group * MOE_SUB_UNROLL, n_sub, single, 0)

            @pl.when(n_unit % 2 == 1)
            def _():
                sub_tile(n_sub * MOE_SUB, MOE_UNIT, wgb[...], wub[...], wdb[...])

        @pl.when(f == pl.num_programs(1) - 1)
        def _():
            start, wait = unit_copies(buf_unit(acc), hbm_unit(y_hbm, row0), sem_out, n_unit)
            start()
            wait()


def _moe_experts(xs, visit_expert, visit_row, visit_units, tail, wg, wu, wd):
    p_rows = xs.shape[0]
    n_visits = visit_expert.shape[0]
    _, d, hidden = wg.shape
    assert xs.shape[1] == d and xs.dtype == BF16
    cols = min(MOE_COLS, hidden)
    assert hidden % cols == 0
    n_f = hidden // cols
    slab = MOE_UNITS_PER_VISIT * MOE_UNIT

    def col_tile(v, f, vs):
        return jnp.where(vs[v] > 0, f, n_f - 1)

    est = slab * d * (2 + 4) + 2 * 3 * d * cols * 4 + 3 * d * cols * 2 + 4 * MOE_SUB * d * 4
    grid_spec = pltpu.PrefetchScalarGridSpec(
        num_scalar_prefetch=4,
        grid=(n_visits, n_f),
        in_specs=[
            pl.BlockSpec(memory_space=pl.ANY),
            pl.BlockSpec((None, d, cols), lambda v, f, ve, vr, vs, tl: (ve[v], 0, col_tile(v, f, vs))),
            pl.BlockSpec((None, d, cols), lambda v, f, ve, vr, vs, tl: (ve[v], 0, col_tile(v, f, vs))),
            pl.BlockSpec((None, cols, d), lambda v, f, ve, vr, vs, tl: (ve[v], col_tile(v, f, vs), 0)),
        ],
        out_specs=pl.BlockSpec(memory_space=pl.ANY),
        scratch_shapes=[
            pltpu.VMEM((slab, d), BF16),
            pltpu.VMEM((slab, d), F32),
            pltpu.VMEM((d, cols), BF16),
            pltpu.VMEM((d, cols), BF16),
            pltpu.VMEM((cols, d), BF16),
            pltpu.SemaphoreType.DMA(()),
            pltpu.SemaphoreType.DMA(()),
        ],
    )
    return pl.pallas_call(
        _moe_kernel,
        grid_spec=grid_spec,
        out_shape=jax.ShapeDtypeStruct((p_rows, d), F32),
        compiler_params=pltpu.CompilerParams(
            dimension_semantics=("arbitrary", "arbitrary"), vmem_limit_bytes=_vmem_limit(est)),
        name="moe_experts",
    )(visit_expert, visit_row, visit_units, tail, xs, wg, wu, wd)


def _combine_kernel(pos_ref, pos_next_ref, xp_ref, xs_ref, gate_ref, g_ref, y_hbm,
                    op_ref, os_ref, ybuf, sems, *, rows, prompt_tiles):
    i = pl.program_id(0)
    n = pl.num_programs(0)
    slot = i % 2

    def issue(tile_slot, table_ref):
        def body(j, carry):
            for k in range(TOP_K):
                pltpu.make_async_copy(y_hbm.at[pl.ds(table_ref[0, 0, TOP_K * j + k], 1)],
                                      ybuf.at[tile_slot, k, pl.ds(j, 1)],
                                      sems.at[tile_slot]).start()
            return carry

        lax.fori_loop(0, rows, body, 0, unroll=ROW_DMA_UNROLL)

    @pl.when(i == 0)
    def _():
        issue(0, pos_ref)

    @pl.when(i + 1 < n)
    def _():
        issue(1 - slot, pos_next_ref)

    pltpu.make_async_copy(ybuf.at[slot], ybuf.at[slot], sems.at[slot]).wait()
    gate = gate_ref[...]
    x = jnp.where(i < prompt_tiles, xp_ref[...], xs_ref[...])
    out = x + gate[:, 0:1] * ybuf[slot, 0] + gate[:, 1:2] * ybuf[slot, 1]
    out = _rmsnorm(out, g_ref[...])

    @pl.when(i < prompt_tiles)
    def _():
        op_ref[...] = out

    @pl.when(i >= prompt_tiles)
    def _():
        os_ref[...] = out


def _combine(x3_p, x3_s, y_sorted, pos_tok, gate_tok, g_final):
    m_p, d = x3_p.shape
    m_s = x3_s.shape[0]
    rows = COMBINE_ROWS
    assert m_p % rows == 0 and m_s == rows
    prompt_tiles = m_p // rows
    tiles = prompt_tiles + 1
    pos_table = pos_tok.reshape(tiles, 1, rows * TOP_K)
    est = 2 * TOP_K * rows * d * 4 + 8 * rows * d * 4
    return pl.pallas_call(
        functools.partial(_combine_kernel, rows=rows, prompt_tiles=prompt_tiles),
        grid=(tiles,),
        in_specs=_tile_table_specs(tiles, rows * TOP_K) + [
            pl.BlockSpec((rows, d), lambda i: (jnp.minimum(i, prompt_tiles - 1), 0)),
            pl.BlockSpec((rows, d), lambda i: (0, 0)),
            pl.BlockSpec((rows, TOP_K), lambda i: (i, 0)),
            pl.BlockSpec((1, d), lambda i: (0, 0)),
            pl.BlockSpec(memory_space=pl.ANY),
        ],
        out_specs=(
            pl.BlockSpec((rows, d), lambda i: (jnp.minimum(i, prompt_tiles - 1), 0)),
            pl.BlockSpec((rows, d), lambda i: (0, 0)),
        ),
        out_shape=(jax.ShapeDtypeStruct((m_p, d), F32), jax.ShapeDtypeStruct((m_s, d), F32)),
        scratch_shapes=[pltpu.VMEM((2, TOP_K, rows, d), F32), pltpu.SemaphoreType.DMA((2,))],
        compiler_params=pltpu.CompilerParams(
            dimension_semantics=("arbitrary",), vmem_limit_bytes=_vmem_limit(est)),
        name="moe_combine",
    )(pos_table, pos_table, x3_p, x3_s, gate_tok, g_final, y_sorted)


def _routing_tables(sel, gates):
    m, n_exp = sel.shape
    p_rows = _round_up(m * TOP_K + n_exp * MOE_UNIT, MOE_SUB)
    counts = jnp.sum(sel, axis=0)
    units = (counts + MOE_UNIT - 1) // MOE_UNIT
    group_rows = units * MOE_UNIT
    group_start = jnp.cumsum(group_rows) - group_rows
    rank = jnp.cumsum(sel, axis=0) - sel
    pos_full = group_start[None, :] + rank
    chosen = sel > 0
    pos_lo = jnp.min(jnp.where(chosen, pos_full, p_rows), axis=1)
    pos_hi = jnp.max(jnp.where(chosen, pos_full, -1), axis=1)
    pos_tok = jnp.stack([pos_lo, pos_hi], axis=1).astype(jnp.int32)
    gate_lo = jnp.sum(jnp.where(chosen & (pos_full == pos_lo[:, None]), gates, 0.0), axis=1)
    gate_hi = jnp.sum(jnp.where(chosen & (pos_full == pos_hi[:, None]), gates, 0.0), axis=1)
    gate_tok = jnp.stack([gate_lo, gate_hi], axis=1)
    token = jnp.broadcast_to(jnp.arange(m, dtype=jnp.int32)[:, None], (m, TOP_K))
    src = jnp.zeros((p_rows,), jnp.int32).at[pos_tok.reshape(-1)].set(
        token.reshape(-1), unique_indices=True, mode="promise_in_bounds")

    n_visits = n_exp + (p_rows // MOE_UNIT) // MOE_UNITS_PER_VISIT
    visits_per = (units + MOE_UNITS_PER_VISIT - 1) // MOE_UNITS_PER_VISIT
    visit_end = jnp.cumsum(visits_per)
    vid = jnp.arange(n_visits, dtype=jnp.int32)
    valid = vid < visit_end[-1]
    owner = jnp.minimum(vid, visit_end[-1] - 1)
    expert = jnp.sum(visit_end[None, :] <= owner[:, None], axis=1).astype(jnp.int32)
    j = vid - (visit_end - visits_per)[expert]
    visit_units = jnp.where(
        valid, jnp.clip(units[expert] - j * MOE_UNITS_PER_VISIT, 0, MOE_UNITS_PER_VISIT), 0)
    visit_row = jnp.where(valid, group_start[expert] + j * MOE_UNITS_PER_VISIT * MOE_UNIT, 0)
    total = jnp.sum(group_rows)
    tail = jnp.stack([total, (p_rows - total) // MOE_UNIT]).astype(jnp.int32)
    return (src, pos_tok, gate_tok, expert, visit_row.astype(jnp.int32),
            visit_units.astype(jnp.int32), tail)


def _gate_tables(w_s, b_s, seq_len, width):
    heads = w_s.shape[0]
    length = min(seq_len, GATE_CHUNK)
    pos = jnp.arange(length)
    mask = (pos[:, None] // CAUSAL_CHUNK) >= (pos[None, :] // CAUSAL_CHUNK)
    w = jnp.where(mask[None], w_s[:, :length, :length], 0.0)
    reps = GATE_CHUNK // length
    if reps > 1:
        w = jnp.einsum("ab,hts->hatbs", jnp.eye(reps, dtype=w.dtype), w).reshape(
            heads, GATE_CHUNK, GATE_CHUNK)
    bias = jnp.tile(b_s[:, :length].T, (reps, 1))
    bias = jnp.repeat(bias, width // heads, axis=1)
    return w.astype(BF16), bias


def kernel(x_prompt, x_sample, cache_conv_a, cache_conv_c, l0_norm_mix, l0_w_in, l0_b_in, l0_conv_w, l0_conv_b, l0_ln_a_g, l0_ln_a_b, l0_ln_v_g, l0_ln_v_b, l0_w_s, l0_b_s, l0_w_out, l0_norm_ffn, l0_ffn_gate, l0_ffn_up, l0_ffn_down, l1_norm_mix, l1_w_in, l1_conv_w, l1_w_out, l1_norm_ffn, l1_router, l1_moe_gate, l1_moe_up, l1_moe_down, final_norm):
    n_p, t_p, d = x_prompt.shape
    n_s, t_s, _ = x_sample.shape
    m_p, m_s = n_p * t_p, n_s * t_s
    a_width = l0_conv_b.shape[0]
    row = lambda vec: vec.reshape(1, -1)
    xp = x_prompt.reshape(m_p, d)
    xs = x_sample.reshape(m_s, d)

    l0_head = (row(l0_norm_mix), l0_w_in.astype(BF16), row(l0_b_in), l0_conv_w, row(l0_conv_b),
               row(l0_ln_a_g), row(l0_ln_a_b), row(l0_ln_v_g), row(l0_ln_v_b))
    wout0 = l0_w_out.astype(BF16)
    zero_a = jnp.zeros((n_p,) + cache_conv_a.shape[1:], F32)
    xp, conv_a_prompt, _ = _l0_mixer(
        xp, zero_a, l0_head + _gate_tables(l0_w_s, l0_b_s, t_p, a_width) + (wout0,),
        n_seq=n_p, seq_len=t_p, keep_v=False)
    xs, conv_a_sample, v_sample = _l0_mixer(
        xs, cache_conv_a, l0_head + _gate_tables(l0_w_s, l0_b_s, t_s, a_width) + (wout0,),
        n_seq=n_s, seq_len=t_s, keep_v=True)

    ffn_w = (row(l0_norm_ffn), l0_ffn_gate.astype(BF16), l0_ffn_up.astype(BF16),
             l0_ffn_down.astype(BF16))
    xp = _ffn(xp, *ffn_w)
    xs = _ffn(xs, *ffn_w)

    l1_w = (row(l1_norm_mix), l1_w_in.astype(BF16), l1_conv_w, l1_w_out.astype(BF16),
            row(l1_norm_ffn), l1_router)
    zero_c = jnp.zeros((n_p,) + cache_conv_c.shape[1:], F32)
    no_tail = jnp.zeros((V7X_SUBLANES, d), F32)
    xs, xn_s, gates_s, sel_s, conv_c_sample = _l1_mixer(
        xs, cache_conv_c, no_tail, l1_w, n_seq=n_s, seq_len=t_s, tail_rows=0)
    xp, xn, gates_p, sel_p, conv_c_prompt = _l1_mixer(
        xp, zero_c, xn_s, l1_w, n_seq=n_p, seq_len=t_p, tail_rows=m_s)

    sel = jnp.concatenate([sel_p, sel_s], axis=0)
    gates = jnp.concatenate([gates_p, gates_s], axis=0)
    src, pos_tok, gate_tok, v_expert, v_row, v_subs, tail = _routing_tables(sel, gates)
    x_sorted = _gather_rows(xn, src, rows=MOE_SUB, out_dtype=BF16)
    y_sorted = _moe_experts(x_sorted, v_expert, v_row, v_subs, tail,
                            l1_moe_gate, l1_moe_up, l1_moe_down)
    y_p, y_s = _combine(xp, xs, y_sorted, pos_tok, gate_tok, row(final_norm))

    return (y_p.reshape(n_p, t_p, d), y_s.reshape(n_s, t_s, d), conv_a_prompt, conv_a_sample,
            v_sample.reshape(n_s, t_s, a_width), conv_c_prompt, conv_c_sample)
```

```python
import functools

import jax
import jax.numpy as jnp
from jax import lax
from jax.experimental import pallas as pl
from jax.experimental.pallas import tpu as pltpu

EPS = 1e-5
CAUSAL_CHUNK = 64
GATE_CHUNK = 128
TOP_K = 2

V7X_SUBLANES = 8
V7X_SCOPED_VMEM_BYTES = 60000 * 1024
COMPILER_TEMP_BYTES = 8 << 20

MIX_ROWS = 256
CONV_ROW_BLOCK = 32
FFN_ROWS = 512
FFN_COLS = 512
MOE_UNIT = 128
MOE_SUB = 2 * MOE_UNIT
MOE_UNITS_PER_VISIT = 18
MOE_STRAIGHT_LINE_UNITS = (MOE_UNITS_PER_VISIT - 1, MOE_UNITS_PER_VISIT)
MOE_SUB_UNROLL = 3
MOE_COLS = 256
MOE_OUT_COLS = 256
COMBINE_ROWS = 128
ROW_DMA_UNROLL = 8

F32 = jnp.float32
BF16 = jnp.bfloat16


def _round_up(n, m):
    return -(-n // m) * m


def _vmem_limit(estimate_bytes):
    return int(min(V7X_SCOPED_VMEM_BYTES, estimate_bytes + COMPILER_TEMP_BYTES))


def _rmsnorm(x, g):
    return x * lax.rsqrt(jnp.mean(x * x, axis=-1, keepdims=True) + EPS) * g


def _layernorm(x, g, b):
    mu = jnp.mean(x, axis=-1, keepdims=True)
    xc = x - mu
    return xc * lax.rsqrt(jnp.mean(xc * xc, axis=-1, keepdims=True) + EPS) * g + b


def _dot_bf16x3(a, b):
    a_hi = a.astype(BF16)
    a_lo = (a - a_hi.astype(F32)).astype(BF16)
    b_hi = b.astype(BF16)
    b_lo = (b - b_hi.astype(F32)).astype(BF16)
    dot = functools.partial(jnp.dot, preferred_element_type=F32)
    return dot(a_hi, b_hi) + (dot(a_hi, b_lo) + dot(a_lo, b_hi))


def _resident(shape):
    nd = len(shape)
    return pl.BlockSpec(shape, lambda *_: (0,) * nd, pipeline_mode=pl.Buffered(1))


def _mixer_tiling(n_seq, seq_len):
    seg_len = min(seq_len, MIX_ROWS)
    n_seg = 1 if seq_len >= MIX_ROWS else n_seq
    tiles_per_seq = seq_len // seg_len if n_seg == 1 else 1
    assert seq_len % seg_len == 0 and (seg_len * n_seg) % (2 * V7X_SUBLANES) == 0
    return seg_len, n_seg, tiles_per_seq, n_seq // n_seg


def _conv_geometry(ksize, seg_len):
    pad = _round_up(ksize - 1, V7X_SUBLANES)
    off = pad - (ksize - 1)
    residues = sorted({(off + k) % V7X_SUBLANES for k in range(ksize)} - {0})
    return pad, off, residues, pad + seg_len - V7X_SUBLANES


def _causal_conv_segments(src, state0_ref, ext_ref, shift_ref, dst_ref, state_out_ref, cw_ref,
                          bias, *, seg_len, n_seg, first_tile):
    ksize = cw_ref.shape[0]
    width = src.shape[-1]
    pad, off, residues, n_shift = _conv_geometry(ksize, seg_len)
    for s in range(n_seg):
        @pl.when(first_tile)
        def _():
            ext_ref[off:pad, :] = state0_ref[s]

        ext_ref[pad:pad + seg_len, :] = src[s * seg_len:(s + 1) * seg_len]
        for i, r in enumerate(residues):
            shift_ref[i] = ext_ref[r:r + n_shift, :]
        for r0 in range(0, seg_len, CONV_ROW_BLOCK):
            rb = min(CONV_ROW_BLOCK, seg_len - r0)
            acc = jnp.broadcast_to(bias, (rb, width))
            for k in range(ksize):
                q, r = divmod(off + k, V7X_SUBLANES)
                lo = q * V7X_SUBLANES + r0
                if r == 0:
                    tap = ext_ref[lo:lo + rb, :]
                else:
                    tap = shift_ref[residues.index(r), lo:lo + rb, :]
                acc = acc + cw_ref[k:k + 1, :] * tap
            dst_ref[s * seg_len + r0:s * seg_len + r0 + rb, :] = acc
        state_out_ref[s] = ext_ref[off + seg_len:pad + seg_len, :]
        if n_seg == 1:
            ext_ref[0:pad, :] = ext_ref[seg_len:seg_len + pad, :]


def _l0_mixer_kernel(x_ref, state0_ref, g_ref, win_ref, bin_ref, cw_ref, cb_ref,
                     lag_ref, lab_ref, lvg_ref, lvb_ref, ws_ref, bs_ref, wout_ref,
                     x1_ref, state_ref, v_ref, ext_ref, shift_ref, y_ref, ab_ref,
                     *, seg_len, n_seg):
    c = cb_ref.shape[-1]
    heads, lc, _ = ws_ref.shape
    hd = c // heads
    rows = seg_len * n_seg

    x = x_ref[...]
    h = _rmsnorm(x, g_ref[...]).astype(BF16)
    z = jnp.dot(h, win_ref[...], preferred_element_type=F32) + bin_ref[...]
    a = z[:, :c] * jax.nn.sigmoid(z[:, c:2 * c])
    u = z[:, 2 * c:3 * c]
    v = z[:, 3 * c:]

    _causal_conv_segments(a, state0_ref, ext_ref, shift_ref, y_ref, state_ref, cw_ref,
                          cb_ref[...], seg_len=seg_len, n_seg=n_seg,
                          first_tile=pl.program_id(1) == 0)
    a_act = _layernorm(y_ref[...], lag_ref[...], lab_ref[...])
    ab_ref[:, :c] = (a_act * jax.nn.sigmoid(a_act)).astype(BF16)

    vn = _layernorm(v, lvg_ref[...], lvb_ref[...])
    v_ref[...] = vn
    vb = vn.astype(BF16)
    for ci in range(rows // lc):
        r = slice(ci * lc, (ci + 1) * lc)
        for hh in range(heads):
            cs = slice(hh * hd, (hh + 1) * hd)
            s = jnp.dot(ws_ref[hh], vb[r, cs], preferred_element_type=F32) + bs_ref[:, cs]
            ab_ref[r, c + hh * hd:c + (hh + 1) * hd] = (u[r, cs] * s).astype(BF16)

    x1_ref[...] = x + jnp.dot(ab_ref[...], wout_ref[...], preferred_element_type=F32)


def _l0_mixer(x, state0, weights, *, n_seq, seq_len, keep_v):
    g, win, b_in, cw, cb, lag, lab, lvg, lvb, ws, bs, wout = weights
    m, d = x.shape
    c = cb.shape[-1]
    ksize = cw.shape[0]
    seg_len, n_seg, tiles_per_seq, groups = _mixer_tiling(n_seq, seq_len)
    rows = seg_len * n_seg
    assert rows % GATE_CHUNK == 0
    pad, _, residues, n_shift = _conv_geometry(ksize, seg_len)

    row_map = lambda i, t: (i * tiles_per_seq + t, 0)
    state_spec = pl.BlockSpec((n_seg, ksize - 1, c), lambda i, t: (i, 0, 0))
    in_specs = [pl.BlockSpec((rows, d), row_map), state_spec] + [_resident(w.shape) for w in weights]
    out_shape = (
        jax.ShapeDtypeStruct((m, d), F32),
        jax.ShapeDtypeStruct((n_seq, ksize - 1, c), F32),
        jax.ShapeDtypeStruct((m if keep_v else rows, c), F32),
    )
    out_specs = (
        pl.BlockSpec((rows, d), row_map),
        state_spec,
        pl.BlockSpec((rows, c), row_map if keep_v else (lambda i, t: (0, 0))),
    )
    est = (2 * (win.size + wout.size) + 4 * rows * d * 4 + 4 * rows * 4 * c * 4
           + 4 * rows * c * 4 + (1 + len(residues)) * (pad + seg_len) * c * 4)
    return pl.pallas_call(
        functools.partial(_l0_mixer_kernel, seg_len=seg_len, n_seg=n_seg),
        grid=(groups, tiles_per_seq),
        in_specs=in_specs,
        out_specs=out_specs,
        out_shape=out_shape,
        scratch_shapes=[
            pltpu.VMEM((pad + seg_len, c), F32),
            pltpu.VMEM((len(residues), n_shift, c), F32),
            pltpu.VMEM((rows, c), F32),
            pltpu.VMEM((rows, 2 * c), BF16),
        ],
        compiler_params=pltpu.CompilerParams(
            dimension_semantics=("arbitrary", "arbitrary"), vmem_limit_bytes=_vmem_limit(est)),
        name="l0_mixer",
    )(x, state0, *weights)


def _ffn_kernel(x_ref, g_ref, wg_ref, wu_ref, wd_ref, o_ref, xn_ref, acc_ref):
    f = pl.program_id(1)

    @pl.when(f == 0)
    def _():
        xn_ref[...] = _rmsnorm(x_ref[...], g_ref[...]).astype(BF16)
        acc_ref[...] = jnp.zeros_like(acc_ref)

    xn = xn_ref[...]
    gate = jnp.dot(xn, wg_ref[...], preferred_element_type=F32)
    up = jnp.dot(xn, wu_ref[...], preferred_element_type=F32)
    hid = (gate * jax.nn.sigmoid(gate) * up).astype(BF16)
    acc_ref[...] += jnp.dot(hid, wd_ref[...], preferred_element_type=F32)

    @pl.when(f == pl.num_programs(1) - 1)
    def _():
        o_ref[...] = x_ref[...] + acc_ref[...]


def _ffn(x, g, wg, wu, wd):
    m, d = x.shape
    hidden = wg.shape[1]
    rows = min(FFN_ROWS, m)
    cols = min(FFN_COLS, hidden)
    assert m % rows == 0 and hidden % cols == 0 and rows % (2 * V7X_SUBLANES) == 0
    est = 4 * rows * d * 4 + rows * d * 4 + rows * d * 2 + 2 * 3 * d * cols * 2 + 3 * rows * cols * 4
    return pl.pallas_call(
        _ffn_kernel,
        grid=(m // rows, hidden // cols),
        in_specs=[
            pl.BlockSpec((rows, d), lambda i, f: (i, 0)),
            pl.BlockSpec((1, d), lambda i, f: (0, 0)),
            pl.BlockSpec((d, cols), lambda i, f: (0, f)),
            pl.BlockSpec((d, cols), lambda i, f: (0, f)),
            pl.BlockSpec((cols, d), lambda i, f: (f, 0)),
        ],
        out_specs=pl.BlockSpec((rows, d), lambda i, f: (i, 0)),
        out_shape=jax.ShapeDtypeStruct((m, d), F32),
        scratch_shapes=[pltpu.VMEM((rows, d), BF16), pltpu.VMEM((rows, d), F32)],
        compiler_params=pltpu.CompilerParams(
            dimension_semantics=("arbitrary", "arbitrary"), vmem_limit_bytes=_vmem_limit(est)),
        name="l0_ffn",
    )(x, g, wg, wu, wd)


def _top2_gates(logits):
    n_exp = logits.shape[-1]
    lane = lax.broadcasted_iota(jnp.int32, logits.shape, 1)
    m1 = jnp.max(logits, axis=-1, keepdims=True)
    i1 = jnp.min(jnp.where(logits == m1, lane, n_exp), axis=-1, keepdims=True)
    sel1 = lane == i1
    rest = jnp.where(sel1, -jnp.inf, logits)
    m2 = jnp.max(rest, axis=-1, keepdims=True)
    i2 = jnp.min(jnp.where(rest == m2, lane, n_exp), axis=-1, keepdims=True)
    sel2 = lane == i2
    e2 = jnp.exp(m2 - m1)
    denom = 1.0 + e2
    gates = jnp.where(sel1, 1.0 / denom, 0.0) + jnp.where(sel2, e2 / denom, 0.0)
    return gates, (sel1 | sel2).astype(jnp.int32)


def _l1_mixer_kernel(x_ref, state0_ref, tail_ref, g_ref, win_ref, cw_ref, wout_ref, gf_ref, rt_ref,
                     x3_ref, xn_ref, gates_ref, sel_ref, state_ref, ext_ref, shift_ref, y_ref,
                     *, seg_len, n_seg, n_groups, tail_rows):
    cw = cw_ref.shape[-1]

    def mixer_tile():
        x = x_ref[...]
        h = _rmsnorm(x, g_ref[...]).astype(BF16)
        z = jnp.dot(h, win_ref[...], preferred_element_type=F32)
        b_g = z[:, :cw]
        p = z[:, cw:2 * cw] * z[:, 2 * cw:]
        _causal_conv_segments(p, state0_ref, ext_ref, shift_ref, y_ref, state_ref, cw_ref,
                              jnp.zeros((1, cw), F32),
                              seg_len=seg_len, n_seg=n_seg, first_tile=pl.program_id(1) == 0)
        q = (b_g * y_ref[...]).astype(BF16)
        x3 = x + jnp.dot(q, wout_ref[...], preferred_element_type=F32)
        x3_ref[...] = x3
        xn = _rmsnorm(x3, gf_ref[...])
        xn_ref[...] = xn
        gates, sel = _top2_gates(_dot_bf16x3(xn, rt_ref[...]))
        gates_ref[...] = gates
        sel_ref[...] = sel

    if tail_rows == 0:
        mixer_tile()
    else:
        pl.when(pl.program_id(0) < n_groups)(mixer_tile)

        @pl.when(jnp.logical_and(pl.program_id(0) == n_groups, pl.program_id(1) == 0))
        def _():
            xn_ref[0:tail_rows, :] = tail_ref[...]


def _l1_mixer(x, state0, xn_tail, weights, *, n_seq, seq_len, tail_rows):
    g, win, cw, wout, gf, router = weights
    m, d = x.shape
    width = cw.shape[-1]
    ksize = cw.shape[0]
    n_exp = router.shape[-1]
    seg_len, n_seg, tiles_per_seq, groups = _mixer_tiling(n_seq, seq_len)
    rows = seg_len * n_seg
    n_blocks = m // rows
    assert tail_rows in (0, xn_tail.shape[0]) and tail_rows <= rows
    pad, _, residues, n_shift = _conv_geometry(ksize, seg_len)

    def row_map(last):
        return lambda i, t: (jnp.minimum(i * tiles_per_seq + t, last), 0)

    state_spec = pl.BlockSpec((n_seg, ksize - 1, width),
                              lambda i, t: (jnp.minimum(i, groups - 1), 0, 0))
    body_rows = pl.BlockSpec((rows, d), row_map(n_blocks - 1))
    in_specs = ([body_rows, state_spec, _resident(xn_tail.shape)]
                + [_resident(w.shape) for w in weights])
    out_shape = (
        jax.ShapeDtypeStruct((m, d), F32),
        jax.ShapeDtypeStruct((m + tail_rows, d), F32),
        jax.ShapeDtypeStruct((m, n_exp), F32),
        jax.ShapeDtypeStruct((m, n_exp), jnp.int32),
        jax.ShapeDtypeStruct((n_seq, ksize - 1, width), F32),
    )
    out_specs = (
        body_rows,
        pl.BlockSpec((rows, d), row_map(n_blocks if tail_rows else n_blocks - 1)),
        pl.BlockSpec((rows, n_exp), row_map(n_blocks - 1)),
        pl.BlockSpec((rows, n_exp), row_map(n_blocks - 1)),
        state_spec,
    )
    est = (2 * (win.size + wout.size) + 6 * rows * d * 4 + 3 * rows * 3 * width * 4
           + 3 * rows * width * 4 + xn_tail.size * 4)
    return pl.pallas_call(
        functools.partial(_l1_mixer_kernel, seg_len=seg_len, n_seg=n_seg, n_groups=groups,
                          tail_rows=tail_rows),
        grid=(groups + (1 if tail_rows else 0), tiles_per_seq),
        in_specs=in_specs,
        out_specs=out_specs,
        out_shape=out_shape,
        scratch_shapes=[
            pltpu.VMEM((pad + seg_len, width), F32),
            pltpu.VMEM((len(residues), n_shift, width), F32),
            pltpu.VMEM((rows, width), F32),
        ],
        compiler_params=pltpu.CompilerParams(
            dimension_semantics=("arbitrary", "arbitrary"), vmem_limit_bytes=_vmem_limit(est)),
        name="l1_mixer",
    )(x, state0, xn_tail, *weights)


def _tile_table_specs(tiles, n):
    return [
        pl.BlockSpec((1, 1, n), lambda i: (i, 0, 0), memory_space=pltpu.SMEM),
        pl.BlockSpec((1, 1, n), lambda i: (jnp.minimum(i + 1, tiles - 1), 0, 0),
                     memory_space=pltpu.SMEM),
    ]


def _gather_rows_kernel(src_ref, src_next_ref, x_hbm, o_ref, buf, sems, *, rows):
    i = pl.program_id(0)
    n = pl.num_programs(0)
    slot = i % 2

    def issue(tile_slot, table_ref):
        def body(j, carry):
            pltpu.make_async_copy(x_hbm.at[pl.ds(table_ref[0, 0, j], 1)],
                                  buf.at[tile_slot, pl.ds(j, 1)], sems.at[tile_slot]).start()
            return carry

        lax.fori_loop(0, rows, body, 0, unroll=ROW_DMA_UNROLL)

    @pl.when(i == 0)
    def _():
        issue(0, src_ref)

    @pl.when(i + 1 < n)
    def _():
        issue(1 - slot, src_next_ref)

    pltpu.make_async_copy(buf.at[slot], buf.at[slot], sems.at[slot]).wait()
    o_ref[...] = buf[slot].astype(o_ref.dtype)


def _gather_rows(x, src, *, rows, out_dtype):
    n_out = src.shape[0]
    width = x.shape[1]
    assert n_out % rows == 0
    tiles = n_out // rows
    return pl.pallas_call(
        functools.partial(_gather_rows_kernel, rows=rows),
        grid=(tiles,),
        in_specs=_tile_table_specs(tiles, rows) + [pl.BlockSpec(memory_space=pl.ANY)],
        out_specs=pl.BlockSpec((rows, width), lambda i: (i, 0)),
        out_shape=jax.ShapeDtypeStruct((n_out, width), out_dtype),
        scratch_shapes=[pltpu.VMEM((2, rows, width), x.dtype), pltpu.SemaphoreType.DMA((2,))],
        compiler_params=pltpu.CompilerParams(dimension_semantics=("arbitrary",)),
        name="moe_gather",
    )(src.reshape(tiles, 1, rows), src.reshape(tiles, 1, rows), x)


def _moe_kernel(ve_ref, vrow_ref, vunit_ref, tail_ref, xs_hbm, wg_ref, wu_ref, wd_ref,
                y_hbm, xbuf, acc, wgb, wub, wdb, sem_in, sem_out):
    del ve_ref
    v = pl.program_id(0)
    f = pl.program_id(1)
    n_unit = vunit_ref[v]
    row0 = vrow_ref[v]

    def rows_at(start, n_rows):
        if isinstance(start, int):
            return pl.ds(start, n_rows)
        return pl.ds(pl.multiple_of(start, MOE_UNIT), n_rows)

    def unit_copies(src_of, dst_of, sem, count):
        def each(method):
            def body(u, carry):
                getattr(pltpu.make_async_copy(src_of(u), dst_of(u), sem), method)()
                return carry
            return lambda: lax.fori_loop(0, count, body, 0)
        return each("start"), each("wait")

    def buf_unit(ref):
        return lambda u: ref.at[rows_at(u * MOE_UNIT, MOE_UNIT)]

    def hbm_unit(ref, base):
        return lambda u: ref.at[rows_at(base + u * MOE_UNIT, MOE_UNIT)]

    def zero_unit(u, carry):
        acc[rows_at(u * MOE_UNIT, MOE_UNIT), :] = jnp.zeros((MOE_UNIT, acc.shape[1]), F32)
        return carry

    @pl.when(jnp.logical_and(v == 0, f == 0))
    def _():
        zero_unit(0, 0)
        start, wait = unit_copies(lambda u: acc.at[rows_at(0, MOE_UNIT)],
                                  hbm_unit(y_hbm, tail_ref[0]), sem_out, tail_ref[1])
        start()
        wait()

    @pl.when(n_unit > 0)
    def _():
        @pl.when(f == 0)
        def _():
            start, wait = unit_copies(hbm_unit(xs_hbm, row0), buf_unit(xbuf), sem_in, n_unit)
            start()
            lax.fori_loop(0, n_unit, zero_unit, 0)
            wait()

        def sub_tile(start, n_rows, wg, wu, wd):
            r = rows_at(start, n_rows)
            xt = xbuf[r, :]
            gate = jnp.dot(xt, wg, preferred_element_type=F32)
            up = jnp.dot(xt, wu, preferred_element_type=F32)
            hid = (gate * jax.nn.sigmoid(gate) * up).astype(BF16)
            acc[r, :] += jnp.dot(hid, wd, preferred_element_type=F32)

        def straight_line(units):
            n_rows = units * MOE_UNIT
            wg = wg_ref[...].astype(BF16)
            wu = wu_ref[...].astype(BF16)
            wd = wd_ref[...].astype(BF16)
            xt = xbuf[0:n_rows, :]
            gate = jnp.dot(xt, wg, preferred_element_type=F32)
            up = jnp.dot(xt, wu, preferred_element_type=F32)
            hid = (gate * jax.nn.sigmoid(gate) * up).astype(BF16)
            for c0 in range(0, acc.shape[1], MOE_OUT_COLS):
                acc[0:n_rows, c0:c0 + MOE_OUT_COLS] += jnp.dot(
                    hid, wd[:, c0:c0 + MOE_OUT_COLS], preferred_element_type=F32)

        for units in MOE_STRAIGHT_LINE_UNITS:
            pl.when(n_unit == units)(functools.partial(straight_line, units))

        @pl.when(n_unit < min(MOE_STRAIGHT_LINE_UNITS))
        def _():
            wgb[...] = wg_ref[...].astype(BF16)
            wub[...] = wu_ref[...].astype(BF16)
            wdb[...] = wd_ref[...].astype(BF16)
            n_sub = n_unit // 2

            def group(i, carry):
                for j in range(MOE_SUB_UNROLL):
                    sub_tile((i * MOE_SUB_UNROLL + j) * MOE_SUB, MOE_SUB, wgb[...], wub[...], wdb[...])
                return carry

            def single(s, carry):
                sub_tile(s * MOE_SUB, MOE_SUB, wgb[...], wub[...], wdb[...])
                return carry

            n_group = n_sub // MOE_SUB_UNROLL
            lax.fori_loop(0, n_group, group, 0)
            lax.fori_loop(n_group * MOE_SUB_UNROLL, n_sub, single, 0)

            @pl.when(n_unit % 2 == 1)
            def _():
                sub_tile(n_sub * MOE_SUB, MOE_UNIT, wgb[...], wub[...], wdb[...])

        @pl.when(f == pl.num_programs(1) - 1)
        def _():
            start, wait = unit_copies(buf_unit(acc), hbm_unit(y_hbm, row0), sem_out, n_unit)
            start()
            wait()


def _moe_experts(xs, visit_expert, visit_row, visit_units, tail, wg, wu, wd):
    p_rows = xs.shape[0]
    n_visits = visit_expert.shape[0]
    _, d, hidden = wg.shape
    assert xs.shape[1] == d and xs.dtype == BF16
    cols = min(MOE_COLS, hidden)
    assert hidden % cols == 0
    n_f = hidden // cols
    slab = MOE_UNITS_PER_VISIT * MOE_UNIT

    def col_tile(v, f, vs):
        return jnp.where(vs[v] > 0, f, n_f - 1)

    est = slab * d * (2 + 4) + 2 * 3 * d * cols * 4 + 3 * d * cols * 2 + 4 * MOE_SUB * d * 4
    grid_spec = pltpu.PrefetchScalarGridSpec(
        num_scalar_prefetch=4,
        grid=(n_visits, n_f),
        in_specs=[
            pl.BlockSpec(memory_space=pl.ANY),
            pl.BlockSpec((None, d, cols), lambda v, f, ve, vr, vs, tl: (ve[v], 0, col_tile(v, f, vs))),
            pl.BlockSpec((None, d, cols), lambda v, f, ve, vr, vs, tl: (ve[v], 0, col_tile(v, f, vs))),
            pl.BlockSpec((None, cols, d), lambda v, f, ve, vr, vs, tl: (ve[v], col_tile(v, f, vs), 0)),
        ],
        out_specs=pl.BlockSpec(memory_space=pl.ANY),
        scratch_shapes=[
            pltpu.VMEM((slab, d), BF16),
            pltpu.VMEM((slab, d), F32),
            pltpu.VMEM((d, cols), BF16),
            pltpu.VMEM((d, cols), BF16),
            pltpu.VMEM((cols, d), BF16),
            pltpu.SemaphoreType.DMA(()),
            pltpu.SemaphoreType.DMA(()),
        ],
    )
    return pl.pallas_call(
        _moe_kernel,
        grid_spec=grid_spec,
        out_shape=jax.ShapeDtypeStruct((p_rows, d), F32),
        compiler_params=pltpu.CompilerParams(
            dimension_semantics=("arbitrary", "arbitrary"), vmem_limit_bytes=_vmem_limit(est)),
        name="moe_experts",
    )(visit_expert, visit_row, visit_units, tail, xs, wg, wu, wd)


def _combine_kernel(pos_ref, pos_next_ref, xp_ref, xs_ref, gate_ref, g_ref, y_hbm,
                    op_ref, os_ref, ybuf, sems, *, rows, prompt_tiles):
    i = pl.program_id(0)
    n = pl.num_programs(0)
    slot = i % 2

    def issue(tile_slot, table_ref):
        def body(j, carry):
            for k in range(TOP_K):
                pltpu.make_async_copy(y_hbm.at[pl.ds(table_ref[0, 0, TOP_K * j + k], 1)],
                                      ybuf.at[tile_slot, k, pl.ds(j, 1)],
                                      sems.at[tile_slot]).start()
            return carry

        lax.fori_loop(0, rows, body, 0, unroll=ROW_DMA_UNROLL)

    @pl.when(i == 0)
    def _():
        issue(0, pos_ref)

    @pl.when(i + 1 < n)
    def _():
        issue(1 - slot, pos_next_ref)

    pltpu.make_async_copy(ybuf.at[slot], ybuf.at[slot], sems.at[slot]).wait()
    gate = gate_ref[...]
    x = jnp.where(i < prompt_tiles, xp_ref[...], xs_ref[...])
    out = x + gate[:, 0:1] * ybuf[slot, 0] + gate[:, 1:2] * ybuf[slot, 1]
    out = _rmsnorm(out, g_ref[...])

    @pl.when(i < prompt_tiles)
    def _():
        op_ref[...] = out

    @pl.when(i >= prompt_tiles)
    def _():
        os_ref[...] = out


def _combine(x3_p, x3_s, y_sorted, pos_tok, gate_tok, g_final):
    m_p, d = x3_p.shape
    m_s = x3_s.shape[0]
    rows = COMBINE_ROWS
    assert m_p % rows == 0 and m_s == rows
    prompt_tiles = m_p // rows
    tiles = prompt_tiles + 1
    pos_table = pos_tok.reshape(tiles, 1, rows * TOP_K)
    est = 2 * TOP_K * rows * d * 4 + 8 * rows * d * 4
    return pl.pallas_call(
        functools.partial(_combine_kernel, rows=rows, prompt_tiles=prompt_tiles),
        grid=(tiles,),
        in_specs=_tile_table_specs(tiles, rows * TOP_K) + [
            pl.BlockSpec((rows, d), lambda i: (jnp.minimum(i, prompt_tiles - 1), 0)),
            pl.BlockSpec((rows, d), lambda i: (0, 0)),
            pl.BlockSpec((rows, TOP_K), lambda i: (i, 0)),
            pl.BlockSpec((1, d), lambda i: (0, 0)),
            pl.BlockSpec(memory_space=pl.ANY),
        ],
        out_specs=(
            pl.BlockSpec((rows, d), lambda i: (jnp.minimum(i, prompt_tiles - 1), 0)),
            pl.BlockSpec((rows, d), lambda i: (0, 0)),
        ),
        out_shape=(jax.ShapeDtypeStruct((m_p, d), F32), jax.ShapeDtypeStruct((m_s, d), F32)),
        scratch_shapes=[pltpu.VMEM((2, TOP_K, rows, d), F32), pltpu.SemaphoreType.DMA((2,))],
        compiler_params=pltpu.CompilerParams(
            dimension_semantics=("arbitrary",), vmem_limit_bytes=_vmem_limit(est)),
        name="moe_combine",
    )(pos_table, pos_table, x3_p, x3_s, gate_tok, g_final, y_sorted)


def _routing_tables(sel, gates):
    m, n_exp = sel.shape
    p_rows = _round_up(m * TOP_K + n_exp * MOE_UNIT, MOE_SUB)
    counts = jnp.sum(sel, axis=0)
    units = (counts + MOE_UNIT - 1) // MOE_UNIT
    group_rows = units * MOE_UNIT
    group_start = jnp.cumsum(group_rows) - group_rows
    rank = jnp.cumsum(sel, axis=0) - sel
    pos_full = group_start[None, :] + rank
    chosen = sel > 0
    pos_lo = jnp.min(jnp.where(chosen, pos_full, p_rows), axis=1)
    pos_hi = jnp.max(jnp.where(chosen, pos_full, -1), axis=1)
    pos_tok = jnp.stack([pos_lo, pos_hi], axis=1).astype(jnp.int32)
    gate_lo = jnp.sum(jnp.where(chosen & (pos_full == pos_lo[:, None]), gates, 0.0), axis=1)
    gate_hi = jnp.sum(jnp.where(chosen & (pos_full == pos_hi[:, None]), gates, 0.0), axis=1)
    gate_tok = jnp.stack([gate_lo, gate_hi], axis=1)
    token = jnp.broadcast_to(jnp.arange(m, dtype=jnp.int32)[:, None], (m, TOP_K))
    src = jnp.zeros((p_rows,), jnp.int32).at[pos_tok.reshape(-1)].set(
        token.reshape(-1), unique_indices=True, mode="promise_in_bounds")

    n_visits = n_exp + (p_rows // MOE_UNIT) // MOE_UNITS_PER_VISIT
    visits_per = (units + MOE_UNITS_PER_VISIT - 1) // MOE_UNITS_PER_VISIT
    visit_end = jnp.cumsum(visits_per)
    vid = jnp.arange(n_visits, dtype=jnp.int32)
    valid = vid < visit_end[-1]
    owner = jnp.minimum(vid, visit_end[-1] - 1)
    expert = jnp.sum(visit_end[None, :] <= owner[:, None], axis=1).astype(jnp.int32)
    j = vid - (visit_end - visits_per)[expert]
    visit_units = jnp.where(
        valid, jnp.clip(units[expert] - j * MOE_UNITS_PER_VISIT, 0, MOE_UNITS_PER_VISIT), 0)
    visit_row = jnp.where(valid, group_start[expert] + j * MOE_UNITS_PER_VISIT * MOE_UNIT, 0)
    total = jnp.sum(group_rows)
    tail = jnp.stack([total, (p_rows - total) // MOE_UNIT]).astype(jnp.int32)
    return (src, pos_tok, gate_tok, expert, visit_row.astype(jnp.int32),
            visit_units.astype(jnp.int32), tail)


def _gate_tables(w_s, b_s, seq_len, width):
    heads = w_s.shape[0]
    length = min(seq_len, GATE_CHUNK)
    pos = jnp.arange(length)
    mask = (pos[:, None] // CAUSAL_CHUNK) >= (pos[None, :] // CAUSAL_CHUNK)
    w = jnp.where(mask[None], w_s[:, :length, :length], 0.0)
    reps = GATE_CHUNK // length
    if reps > 1:
        w = jnp.einsum("ab,hts->hatbs", jnp.eye(reps, dtype=w.dtype), w).reshape(
            heads, GATE_CHUNK, GATE_CHUNK)
    bias = jnp.tile(b_s[:, :length].T, (reps, 1))
    bias = jnp.repeat(bias, width // heads, axis=1)
    return w.astype(BF16), bias


def kernel(x_prompt, x_sample, cache_conv_a, cache_conv_c, l0_norm_mix, l0_w_in, l0_b_in, l0_conv_w, l0_conv_b, l0_ln_a_g, l0_ln_a_b, l0_ln_v_g, l0_ln_v_b, l0_w_s, l0_b_s, l0_w_out, l0_norm_ffn, l0_ffn_gate, l0_ffn_up, l0_ffn_down, l1_norm_mix, l1_w_in, l1_conv_w, l1_w_out, l1_norm_ffn, l1_router, l1_moe_gate, l1_moe_up, l1_moe_down, final_norm):
    n_p, t_p, d = x_prompt.shape
    n_s, t_s, _ = x_sample.shape
    m_p, m_s = n_p * t_p, n_s * t_s
    a_width = l0_conv_b.shape[0]
    row = lambda vec: vec.reshape(1, -1)
    xp = x_prompt.reshape(m_p, d)
    xs = x_sample.reshape(m_s, d)

    l0_head = (row(l0_norm_mix), l0_w_in.astype(BF16), row(l0_b_in), l0_conv_w, row(l0_conv_b),
               row(l0_ln_a_g), row(l0_ln_a_b), row(l0_ln_v_g), row(l0_ln_v_b))
    wout0 = l0_w_out.astype(BF16)
    zero_a = jnp.zeros((n_p,) + cache_conv_a.shape[1:], F32)
    xp, conv_a_prompt, _ = _l0_mixer(
        xp, zero_a, l0_head + _gate_tables(l0_w_s, l0_b_s, t_p, a_width) + (wout0,),
        n_seq=n_p, seq_len=t_p, keep_v=False)
    xs, conv_a_sample, v_sample = _l0_mixer(
        xs, cache_conv_a, l0_head + _gate_tables(l0_w_s, l0_b_s, t_s, a_width) + (wout0,),
        n_seq=n_s, seq_len=t_s, keep_v=True)

    ffn_w = (row(l0_norm_ffn), l0_ffn_gate.astype(BF16), l0_ffn_up.astype(BF16),
             l0_ffn_down.astype(BF16))
    xp = _ffn(xp, *ffn_w)
    xs = _ffn(xs, *ffn_w)

    l1_w = (row(l1_norm_mix), l1_w_in.astype(BF16), l1_conv_w, l1_w_out.astype(BF16),
            row(l1_norm_ffn), l1_router)
    zero_c = jnp.zeros((n_p,) + cache_conv_c.shape[1:], F32)
    no_tail = jnp.zeros((V7X_SUBLANES, d), F32)
    xs, xn_s, gates_s, sel_s, conv_c_sample = _l1_mixer(
        xs, cache_conv_c, no_tail, l1_w, n_seq=n_s, seq_len=t_s, tail_rows=0)
    xp, xn, gates_p, sel_p, conv_c_prompt = _l1_mixer(
        xp, zero_c, xn_s, l1_w, n_seq=n_p, seq_len=t_p, tail_rows=m_s)

    sel = jnp.concatenate([sel_p, sel_s], axis=0)
    gates = jnp.concatenate([gates_p, gates_s], axis=0)
    src, pos_tok, gate_tok, v_expert, v_row, v_subs, tail = _routing_tables(sel, gates)
    x_sorted = _gather_rows(xn, src, rows=MOE_SUB, out_dtype=BF16)
    y_sorted = _moe_experts(x_sorted, v_expert, v_row, v_subs, tail,
                            l1_moe_gate, l1_moe_up, l1_moe_down)
    y_p, y_s = _combine(xp, xs, y_sorted, pos_tok, gate_tok, row(final_norm))

    return (y_p.reshape(n_p, t_p, d), y_s.reshape(n_s, t_s, d), conv_a_prompt, conv_a_sample,
            v_sample.reshape(n_s, t_s, a_width), conv_c_prompt, conv_c_sample)
```

```python
import functools

import jax
import jax.numpy as jnp
from jax import lax
from jax.experimental import pallas as pl
from jax.experimental.pallas import tpu as pltpu

EPS = 1e-5
CAUSAL_CHUNK = 64
GATE_CHUNK = 128
TOP_K = 2

V7X_SUBLANES = 8
V7X_SCOPED_VMEM_BYTES = 60000 * 1024
COMPILER_TEMP_BYTES = 8 << 20

MIX_ROWS = 256
CONV_ROW_BLOCK = 256
FFN_ROWS = 512
FFN_COLS = 512
MOE_UNIT = 128
MOE_SUB = 2 * MOE_UNIT
MOE_UNITS_PER_VISIT = 18
MOE_STRAIGHT_LINE_UNITS = (MOE_UNITS_PER_VISIT - 1, MOE_UNITS_PER_VISIT)
MOE_SUB_UNROLL = 3
MOE_COLS = 256
MOE_OUT_COLS = 256
COMBINE_ROWS = 128
ROW_DMA_UNROLL = 8

F32 = jnp.float32
BF16 = jnp.bfloat16


def _round_up(n, m):
    return -(-n // m) * m


def _vmem_limit(estimate_bytes):
    return int(min(V7X_SCOPED_VMEM_BYTES, estimate_bytes + COMPILER_TEMP_BYTES))


def _rmsnorm(x, g):
    return x * lax.rsqrt(jnp.mean(x * x, axis=-1, keepdims=True) + EPS) * g


def _layernorm(x, g, b):
    mu = jnp.mean(x, axis=-1, keepdims=True)
    xc = x - mu
    return xc * lax.rsqrt(jnp.mean(xc * xc, axis=-1, keepdims=True) + EPS) * g + b


def _dot_bf16x3(a, b):
    a_hi = a.astype(BF16)
    a_lo = (a - a_hi.astype(F32)).astype(BF16)
    b_hi = b.astype(BF16)
    b_lo = (b - b_hi.astype(F32)).astype(BF16)
    dot = functools.partial(jnp.dot, preferred_element_type=F32)
    return dot(a_hi, b_hi) + (dot(a_hi, b_lo) + dot(a_lo, b_hi))


def _resident(shape):
    nd = len(shape)
    return pl.BlockSpec(shape, lambda *_: (0,) * nd, pipeline_mode=pl.Buffered(1))


def _mixer_tiling(n_seq, seq_len):
    seg_len = min(seq_len, MIX_ROWS)
    n_seg = 1 if seq_len >= MIX_ROWS else n_seq
    tiles_per_seq = seq_len // seg_len if n_seg == 1 else 1
    assert seq_len % seg_len == 0 and (seg_len * n_seg) % (2 * V7X_SUBLANES) == 0
    return seg_len, n_seg, tiles_per_seq, n_seq // n_seg


def _conv_geometry(ksize, seg_len):
    pad = _round_up(ksize - 1, V7X_SUBLANES)
    off = pad - (ksize - 1)
    residues = sorted({(off + k) % V7X_SUBLANES for k in range(ksize)} - {0})
    return pad, off, residues, pad + seg_len - V7X_SUBLANES


def _causal_conv_segments(src, state0_ref, ext_ref, shift_ref, dst_ref, state_out_ref, cw_ref,
                          bias, *, seg_len, n_seg, first_tile):
    ksize = cw_ref.shape[0]
    width = src.shape[-1]
    pad, off, residues, n_shift = _conv_geometry(ksize, seg_len)
    for s in range(n_seg):
        @pl.when(first_tile)
        def _():
            ext_ref[off:pad, :] = state0_ref[s]

        ext_ref[pad:pad + seg_len, :] = src[s * seg_len:(s + 1) * seg_len]
        for i, r in enumerate(residues):
            shift_ref[i] = ext_ref[r:r + n_shift, :]
        for r0 in range(0, seg_len, CONV_ROW_BLOCK):
            rb = min(CONV_ROW_BLOCK, seg_len - r0)
            acc = jnp.broadcast_to(bias, (rb, width))
            for k in range(ksize):
                q, r = divmod(off + k, V7X_SUBLANES)
                lo = q * V7X_SUBLANES + r0
                if r == 0:
                    tap = ext_ref[lo:lo + rb, :]
                else:
                    tap = shift_ref[residues.index(r), lo:lo + rb, :]
                acc = acc + cw_ref[k:k + 1, :] * tap
            dst_ref[s * seg_len + r0:s * seg_len + r0 + rb, :] = acc
        state_out_ref[s] = ext_ref[off + seg_len:pad + seg_len, :]
        if n_seg == 1:
            ext_ref[0:pad, :] = ext_ref[seg_len:seg_len + pad, :]


def _l0_mixer_kernel(x_ref, state0_ref, g_ref, win_ref, bin_ref, cw_ref, cb_ref,
                     lag_ref, lab_ref, lvg_ref, lvb_ref, ws_ref, bs_ref, wout_ref,
                     x1_ref, state_ref, v_ref, ext_ref, shift_ref, y_ref, ab_ref,
                     *, seg_len, n_seg):
    c = cb_ref.shape[-1]
    heads, lc, _ = ws_ref.shape
    hd = c // heads
    rows = seg_len * n_seg

    x = x_ref[...]
    h = _rmsnorm(x, g_ref[...]).astype(BF16)
    z = jnp.dot(h, win_ref[...], preferred_element_type=F32) + bin_ref[...]
    a = z[:, :c] * jax.nn.sigmoid(z[:, c:2 * c])
    u = z[:, 2 * c:3 * c]
    v = z[:, 3 * c:]

    _causal_conv_segments(a, state0_ref, ext_ref, shift_ref, y_ref, state_ref, cw_ref,
                          cb_ref[...], seg_len=seg_len, n_seg=n_seg,
                          first_tile=pl.program_id(1) == 0)
    a_act = _layernorm(y_ref[...], lag_ref[...], lab_ref[...])
    ab_ref[:, :c] = (a_act * jax.nn.sigmoid(a_act)).astype(BF16)

    vn = _layernorm(v, lvg_ref[...], lvb_ref[...])
    v_ref[...] = vn
    vb = vn.astype(BF16)
    for ci in range(rows // lc):
        r = slice(ci * lc, (ci + 1) * lc)
        for hh in range(heads):
            cs = slice(hh * hd, (hh + 1) * hd)
            s = jnp.dot(ws_ref[hh], vb[r, cs], preferred_element_type=F32) + bs_ref[:, cs]
            ab_ref[r, c + hh * hd:c + (hh + 1) * hd] = (u[r, cs] * s).astype(BF16)

    x1_ref[...] = x + jnp.dot(ab_ref[...], wout_ref[...], preferred_element_type=F32)


def _l0_mixer(x, state0, weights, *, n_seq, seq_len, keep_v):
    g, win, b_in, cw, cb, lag, lab, lvg, lvb, ws, bs, wout = weights
    m, d = x.shape
    c = cb.shape[-1]
    ksize = cw.shape[0]
    seg_len, n_seg, tiles_per_seq, groups = _mixer_tiling(n_seq, seq_len)
    rows = seg_len * n_seg
    assert rows % GATE_CHUNK == 0
    pad, _, residues, n_shift = _conv_geometry(ksize, seg_len)

    row_map = lambda i, t: (i * tiles_per_seq + t, 0)
    state_spec = pl.BlockSpec((n_seg, ksize - 1, c), lambda i, t: (i, 0, 0))
    in_specs = [pl.BlockSpec((rows, d), row_map), state_spec] + [_resident(w.shape) for w in weights]
    out_shape = (
        jax.ShapeDtypeStruct((m, d), F32),
        jax.ShapeDtypeStruct((n_seq, ksize - 1, c), F32),
        jax.ShapeDtypeStruct((m if keep_v else rows, c), F32),
    )
    out_specs = (
        pl.BlockSpec((rows, d), row_map),
        state_spec,
        pl.BlockSpec((rows, c), row_map if keep_v else (lambda i, t: (0, 0))),
    )
    est = (2 * (win.size + wout.size) + 4 * rows * d * 4 + 4 * rows * 4 * c * 4
           + 4 * rows * c * 4 + (1 + len(residues)) * (pad + seg_len) * c * 4)
    return pl.pallas_call(
        functools.partial(_l0_mixer_kernel, seg_len=seg_len, n_seg=n_seg),
        grid=(groups, tiles_per_seq),
        in_specs=in_specs,
        out_specs=out_specs,
        out_shape=out_shape,
        scratch_shapes=[
            pltpu.VMEM((pad + seg_len, c), F32),
            pltpu.VMEM((len(residues), n_shift, c), F32),
            pltpu.VMEM((rows, c), F32),
            pltpu.VMEM((rows, 2 * c), BF16),
        ],
        compiler_params=pltpu.CompilerParams(
            dimension_semantics=("arbitrary", "arbitrary"), vmem_limit_bytes=_vmem_limit(est)),
        name="l0_mixer",
    )(x, state0, *weights)


def _ffn_kernel(x_ref, g_ref, wg_ref, wu_ref, wd_ref, o_ref, xn_ref, acc_ref):
    f = pl.program_id(1)

    @pl.when(f == 0)
    def _():
        xn_ref[...] = _rmsnorm(x_ref[...], g_ref[...]).astype(BF16)
        acc_ref[...] = jnp.zeros_like(acc_ref)

    xn = xn_ref[...]
    gate = jnp.dot(xn, wg_ref[...], preferred_element_type=F32)
    up = jnp.dot(xn, wu_ref[...], preferred_element_type=F32)
    hid = (gate * jax.nn.sigmoid(gate) * up).astype(BF16)
    acc_ref[...] += jnp.dot(hid, wd_ref[...], preferred_element_type=F32)

    @pl.when(f == pl.num_programs(1) - 1)
    def _():
        o_ref[...] = x_ref[...] + acc_ref[...]


def _ffn(x, g, wg, wu, wd):
    m, d = x.shape
    hidden = wg.shape[1]
    rows = min(FFN_ROWS, m)
    cols = min(FFN_COLS, hidden)
    assert m % rows == 0 and hidden % cols == 0 and rows % (2 * V7X_SUBLANES) == 0
    est = 4 * rows * d * 4 + rows * d * 4 + rows * d * 2 + 2 * 3 * d * cols * 2 + 3 * rows * cols * 4
    return pl.pallas_call(
        _ffn_kernel,
        grid=(m // rows, hidden // cols),
        in_specs=[
            pl.BlockSpec((rows, d), lambda i, f: (i, 0)),
            pl.BlockSpec((1, d), lambda i, f: (0, 0)),
            pl.BlockSpec((d, cols), lambda i, f: (0, f)),
            pl.BlockSpec((d, cols), lambda i, f: (0, f)),
            pl.BlockSpec((cols, d), lambda i, f: (f, 0)),
        ],
        out_specs=pl.BlockSpec((rows, d), lambda i, f: (i, 0)),
        out_shape=jax.ShapeDtypeStruct((m, d), F32),
        scratch_shapes=[pltpu.VMEM((rows, d), BF16), pltpu.VMEM((rows, d), F32)],
        compiler_params=pltpu.CompilerParams(
            dimension_semantics=("arbitrary", "arbitrary"), vmem_limit_bytes=_vmem_limit(est)),
        name="l0_ffn",
    )(x, g, wg, wu, wd)


def _top2_gates(logits):
    n_exp = logits.shape[-1]
    lane = lax.broadcasted_iota(jnp.int32, logits.shape, 1)
    m1 = jnp.max(logits, axis=-1, keepdims=True)
    i1 = jnp.min(jnp.where(logits == m1, lane, n_exp), axis=-1, keepdims=True)
    sel1 = lane == i1
    rest = jnp.where(sel1, -jnp.inf, logits)
    m2 = jnp.max(rest, axis=-1, keepdims=True)
    i2 = jnp.min(jnp.where(rest == m2, lane, n_exp), axis=-1, keepdims=True)
    sel2 = lane == i2
    e2 = jnp.exp(m2 - m1)
    denom = 1.0 + e2
    gates = jnp.where(sel1, 1.0 / denom, 0.0) + jnp.where(sel2, e2 / denom, 0.0)
    return gates, (sel1 | sel2).astype(jnp.int32)


def _l1_mixer_kernel(x_ref, state0_ref, tail_ref, g_ref, win_ref, cw_ref, wout_ref, gf_ref, rt_ref,
                     x3_ref, xn_ref, gates_ref, sel_ref, state_ref, ext_ref, shift_ref, y_ref,
                     *, seg_len, n_seg, n_groups, tail_rows):
    cw = cw_ref.shape[-1]

    def mixer_tile():
        x = x_ref[...]
        h = _rmsnorm(x, g_ref[...]).astype(BF16)
        z = jnp.dot(h, win_ref[...], preferred_element_type=F32)
        b_g = z[:, :cw]
        p = z[:, cw:2 * cw] * z[:, 2 * cw:]
        _causal_conv_segments(p, state0_ref, ext_ref, shift_ref, y_ref, state_ref, cw_ref,
                              jnp.zeros((1, cw), F32),
                              seg_len=seg_len, n_seg=n_seg, first_tile=pl.program_id(1) == 0)
        q = (b_g * y_ref[...]).astype(BF16)
        x3 = x + jnp.dot(q, wout_ref[...], preferred_element_type=F32)
        x3_ref[...] = x3
        xn = _rmsnorm(x3, gf_ref[...])
        xn_ref[...] = xn
        gates, sel = _top2_gates(_dot_bf16x3(xn, rt_ref[...]))
        gates_ref[...] = gates
        sel_ref[...] = sel

    if tail_rows == 0:
        mixer_tile()
    else:
        pl.when(pl.program_id(0) < n_groups)(mixer_tile)

        @pl.when(jnp.logical_and(pl.program_id(0) == n_groups, pl.program_id(1) == 0))
        def _():
            xn_ref[0:tail_rows, :] = tail_ref[...]


def _l1_mixer(x, state0, xn_tail, weights, *, n_seq, seq_len, tail_rows):
    g, win, cw, wout, gf, router = weights
    m, d = x.shape
    width = cw.shape[-1]
    ksize = cw.shape[0]
    n_exp = router.shape[-1]
    seg_len, n_seg, tiles_per_seq, groups = _mixer_tiling(n_seq, seq_len)
    rows = seg_len * n_seg
    n_blocks = m // rows
    assert tail_rows in (0, xn_tail.shape[0]) and tail_rows <= rows
    pad, _, residues, n_shift = _conv_geometry(ksize, seg_len)

    def row_map(last):
        return lambda i, t: (jnp.minimum(i * tiles_per_seq + t, last), 0)

    state_spec = pl.BlockSpec((n_seg, ksize - 1, width),
                              lambda i, t: (jnp.minimum(i, groups - 1), 0, 0))
    body_rows = pl.BlockSpec((rows, d), row_map(n_blocks - 1))
    in_specs = ([body_rows, state_spec, _resident(xn_tail.shape)]
                + [_resident(w.shape) for w in weights])
    out_shape = (
        jax.ShapeDtypeStruct((m, d), F32),
        jax.ShapeDtypeStruct((m + tail_rows, d), F32),
        jax.ShapeDtypeStruct((m, n_exp), F32),
        jax.ShapeDtypeStruct((m, n_exp), jnp.int32),
        jax.ShapeDtypeStruct((n_seq, ksize - 1, width), F32),
    )
    out_specs = (
        body_rows,
        pl.BlockSpec((rows, d), row_map(n_blocks if tail_rows else n_blocks - 1)),
        pl.BlockSpec((rows, n_exp), row_map(n_blocks - 1)),
        pl.BlockSpec((rows, n_exp), row_map(n_blocks - 1)),
        state_spec,
    )
    est = (2 * (win.size + wout.size) + 6 * rows * d * 4 + 3 * rows * 3 * width * 4
           + 3 * rows * width * 4 + xn_tail.size * 4)
    return pl.pallas_call(
        functools.partial(_l1_mixer_kernel, seg_len=seg_len, n_seg=n_seg, n_groups=groups,
                          tail_rows=tail_rows),
        grid=(groups + (1 if tail_rows else 0), tiles_per_seq),
        in_specs=in_specs,
        out_specs=out_specs,
        out_shape=out_shape,
        scratch_shapes=[
            pltpu.VMEM((pad + seg_len, width), F32),
            pltpu.VMEM((len(residues), n_shift, width), F32),
            pltpu.VMEM((rows, width), F32),
        ],
        compiler_params=pltpu.CompilerParams(
            dimension_semantics=("arbitrary", "arbitrary"), vmem_limit_bytes=_vmem_limit(est)),
        name="l1_mixer",
    )(x, state0, xn_tail, *weights)


def _tile_table_specs(tiles, n):
    return [
        pl.BlockSpec((1, 1, n), lambda i: (i, 0, 0), memory_space=pltpu.SMEM),
        pl.BlockSpec((1, 1, n), lambda i: (jnp.minimum(i + 1, tiles - 1), 0, 0),
                     memory_space=pltpu.SMEM),
    ]


def _gather_rows_kernel(src_ref, src_next_ref, x_hbm, o_ref, buf, sems, *, rows):
    i = pl.program_id(0)
    n = pl.num_programs(0)
    slot = i % 2

    def issue(tile_slot, table_ref):
        def body(j, carry):
            pltpu.make_async_copy(x_hbm.at[pl.ds(table_ref[0, 0, j], 1)],
                                  buf.at[tile_slot, pl.ds(j, 1)], sems.at[tile_slot]).start()
            return carry

        lax.fori_loop(0, rows, body, 0, unroll=ROW_DMA_UNROLL)

    @pl.when(i == 0)
    def _():
        issue(0, src_ref)

    @pl.when(i + 1 < n)
    def _():
        issue(1 - slot, src_next_ref)

    pltpu.make_async_copy(buf.at[slot], buf.at[slot], sems.at[slot]).wait()
    o_ref[...] = buf[slot].astype(o_ref.dtype)


def _gather_rows(x, src, *, rows, out_dtype):
    n_out = src.shape[0]
    width = x.shape[1]
    assert n_out % rows == 0
    tiles = n_out // rows
    return pl.pallas_call(
        functools.partial(_gather_rows_kernel, rows=rows),
        grid=(tiles,),
        in_specs=_tile_table_specs(tiles, rows) + [pl.BlockSpec(memory_space=pl.ANY)],
        out_specs=pl.BlockSpec((rows, width), lambda i: (i, 0)),
        out_shape=jax.ShapeDtypeStruct((n_out, width), out_dtype),
        scratch_shapes=[pltpu.VMEM((2, rows, width), x.dtype), pltpu.SemaphoreType.DMA((2,))],
        compiler_params=pltpu.CompilerParams(dimension_semantics=("arbitrary",)),
        name="moe_gather",
    )(src.reshape(tiles, 1, rows), src.reshape(tiles, 1, rows), x)


def _moe_kernel(ve_ref, vrow_ref, vunit_ref, tail_ref, xs_hbm, wg_ref, wu_ref, wd_ref,
                y_hbm, xbuf, acc, wgb, wub, wdb, sem_in, sem_out):
    del ve_ref
    v = pl.program_id(0)
    f = pl.program_id(1)
    n_unit = vunit_ref[v]
    row0 = vrow_ref[v]

    def rows_at(start, n_rows):
        if isinstance(start, int):
            return pl.ds(start, n_rows)
        return pl.ds(pl.multiple_of(start, MOE_UNIT), n_rows)

    def unit_copies(src_of, dst_of, sem, count):
        def each(method):
            def body(u, carry):
                getattr(pltpu.make_async_copy(src_of(u), dst_of(u), sem), method)()
                return carry
            return lambda: lax.fori_loop(0, count, body, 0)
        return each("start"), each("wait")

    def buf_unit(ref):
        return lambda u: ref.at[rows_at(u * MOE_UNIT, MOE_UNIT)]

    def hbm_unit(ref, base):
        return lambda u: ref.at[rows_at(base + u * MOE_UNIT, MOE_UNIT)]

    def zero_unit(u, carry):
        acc[rows_at(u * MOE_UNIT, MOE_UNIT), :] = jnp.zeros((MOE_UNIT, acc.shape[1]), F32)
        return carry

    @pl.when(jnp.logical_and(v == 0, f == 0))
    def _():
        zero_unit(0, 0)
        start, wait = unit_copies(lambda u: acc.at[rows_at(0, MOE_UNIT)],
                                  hbm_unit(y_hbm, tail_ref[0]), sem_out, tail_ref[1])
        start()
        wait()

    @pl.when(n_unit > 0)
    def _():
        @pl.when(f == 0)
        def _():
            start, wait = unit_copies(hbm_unit(xs_hbm, row0), buf_unit(xbuf), sem_in, n_unit)
            start()
            lax.fori_loop(0, n_unit, zero_unit, 0)
            wait()

        def sub_tile(start, n_rows, wg, wu, wd):
            r = rows_at(start, n_rows)
            xt = xbuf[r, :]
            gate = jnp.dot(xt, wg, preferred_element_type=F32)
            up = jnp.dot(xt, wu, preferred_element_type=F32)
            hid = (gate * jax.nn.sigmoid(gate) * up).astype(BF16)
            acc[r, :] += jnp.dot(hid, wd, preferred_element_type=F32)

        def straight_line(units):
            n_rows = units * MOE_UNIT
            wg = wg_ref[...].astype(BF16)
            wu = wu_ref[...].astype(BF16)
            wd = wd_ref[...].astype(BF16)
            xt = xbuf[0:n_rows, :]
            gate = jnp.dot(xt, wg, preferred_element_type=F32)
            up = jnp.dot(xt, wu, preferred_element_type=F32)
            hid = (gate * jax.nn.sigmoid(gate) * up).astype(BF16)
            for c0 in range(0, acc.shape[1], MOE_OUT_COLS):
                acc[0:n_rows, c0:c0 + MOE_OUT_COLS] += jnp.dot(
                    hid, wd[:, c0:c0 + MOE_OUT_COLS], preferred_element_type=F32)

        for units in MOE_STRAIGHT_LINE_UNITS:
            pl.when(n_unit == units)(functools.partial(straight_line, units))

        @pl.when(n_unit < min(MOE_STRAIGHT_LINE_UNITS))
        def _():
            wgb[...] = wg_ref[...].astype(BF16)
            wub[...] = wu_ref[...].astype(BF16)
            wdb[...] = wd_ref[...].astype(BF16)
            n_sub = n_unit // 2

            def group(i, carry):
                for j in range(MOE_SUB_UNROLL):
                    sub_tile((i * MOE_SUB_UNROLL + j) * MOE_SUB, MOE_SUB, wgb[...], wub[...], wdb[...])
                return carry

            def single(s, carry):
                sub_tile(s * MOE_SUB, MOE_SUB, wgb[...], wub[...], wdb[...])
                return carry

            n_group = n_sub // MOE_SUB_UNROLL
            lax.fori_loop(0, n_group, group, 0)
            lax.fori_loop(n_group * MOE_SUB_UNROLL, n_sub, single, 0)

            @pl.when(n_unit % 2 == 1)
            def _():
                sub_tile(n_sub * MOE_SUB, MOE_UNIT, wgb[...], wub[...], wdb[...])

        @pl.when(f == pl.num_programs(1) - 1)
        def _():
            start, wait = unit_copies(buf_unit(acc), hbm_unit(y_hbm, row0), sem_out, n_unit)
            start()
            wait()


def _moe_experts(xs, visit_expert, visit_row, visit_units, tail, wg, wu, wd):
    p_rows = xs.shape[0]
    n_visits = visit_expert.shape[0]
    _, d, hidden = wg.shape
    assert xs.shape[1] == d and xs.dtype == BF16
    cols = min(MOE_COLS, hidden)
    assert hidden % cols == 0
    n_f = hidden // cols
    slab = MOE_UNITS_PER_VISIT * MOE_UNIT

    def col_tile(v, f, vs):
        return jnp.where(vs[v] > 0, f, n_f - 1)

    est = slab * d * (2 + 4) + 2 * 3 * d * cols * 4 + 3 * d * cols * 2 + 4 * MOE_SUB * d * 4
    grid_spec = pltpu.PrefetchScalarGridSpec(
        num_scalar_prefetch=4,
        grid=(n_visits, n_f),
        in_specs=[
            pl.BlockSpec(memory_space=pl.ANY),
            pl.BlockSpec((None, d, cols), lambda v, f, ve, vr, vs, tl: (ve[v], 0, col_tile(v, f, vs))),
            pl.BlockSpec((None, d, cols), lambda v, f, ve, vr, vs, tl: (ve[v], 0, col_tile(v, f, vs))),
            pl.BlockSpec((None, cols, d), lambda v, f, ve, vr, vs, tl: (ve[v], col_tile(v, f, vs), 0)),
        ],
        out_specs=pl.BlockSpec(memory_space=pl.ANY),
        scratch_shapes=[
            pltpu.VMEM((slab, d), BF16),
            pltpu.VMEM((slab, d), F32),
            pltpu.VMEM((d, cols), BF16),
            pltpu.VMEM((d, cols), BF16),
            pltpu.VMEM((cols, d), BF16),
            pltpu.SemaphoreType.DMA(()),
            pltpu.SemaphoreType.DMA(()),
        ],
    )
    return pl.pallas_call(
        _moe_kernel,
        grid_spec=grid_spec,
        out_shape=jax.ShapeDtypeStruct((p_rows, d), F32),
        compiler_params=pltpu.CompilerParams(
            dimension_semantics=("arbitrary", "arbitrary"), vmem_limit_bytes=_vmem_limit(est)),
        name="moe_experts",
    )(visit_expert, visit_row, visit_units, tail, xs, wg, wu, wd)


def _combine_kernel(pos_ref, pos_next_ref, xp_ref, xs_ref, gate_ref, g_ref, y_hbm,
                    op_ref, os_ref, ybuf, sems, *, rows, prompt_tiles):
    i = pl.program_id(0)
    n = pl.num_programs(0)
    slot = i % 2

    def issue(tile_slot, table_ref):
        def body(j, carry):
            for k in range(TOP_K):
                pltpu.make_async_copy(y_hbm.at[pl.ds(table_ref[0, 0, TOP_K * j + k], 1)],
                                      ybuf.at[tile_slot, k, pl.ds(j, 1)],
                                      sems.at[tile_slot]).start()
            return carry

        lax.fori_loop(0, rows, body, 0, unroll=ROW_DMA_UNROLL)

    @pl.when(i == 0)
    def _():
        issue(0, pos_ref)

    @pl.when(i + 1 < n)
    def _():
        issue(1 - slot, pos_next_ref)

    pltpu.make_async_copy(ybuf.at[slot], ybuf.at[slot], sems.at[slot]).wait()
    gate = gate_ref[...]
    x = jnp.where(i < prompt_tiles, xp_ref[...], xs_ref[...])
    out = x + gate[:, 0:1] * ybuf[slot, 0] + gate[:, 1:2] * ybuf[slot, 1]
    out = _rmsnorm(out, g_ref[...])

    @pl.when(i < prompt_tiles)
    def _():
        op_ref[...] = out

    @pl.when(i >= prompt_tiles)
    def _():
        os_ref[...] = out


def _combine(x3_p, x3_s, y_sorted, pos_tok, gate_tok, g_final):
    m_p, d = x3_p.shape
    m_s = x3_s.shape[0]
    rows = COMBINE_ROWS
    assert m_p % rows == 0 and m_s == rows
    prompt_tiles = m_p // rows
    tiles = prompt_tiles + 1
    pos_table = pos_tok.reshape(tiles, 1, rows * TOP_K)
    est = 2 * TOP_K * rows * d * 4 + 8 * rows * d * 4
    return pl.pallas_call(
        functools.partial(_combine_kernel, rows=rows, prompt_tiles=prompt_tiles),
        grid=(tiles,),
        in_specs=_tile_table_specs(tiles, rows * TOP_K) + [
            pl.BlockSpec((rows, d), lambda i: (jnp.minimum(i, prompt_tiles - 1), 0)),
            pl.BlockSpec((rows, d), lambda i: (0, 0)),
            pl.BlockSpec((rows, TOP_K), lambda i: (i, 0)),
            pl.BlockSpec((1, d), lambda i: (0, 0)),
            pl.BlockSpec(memory_space=pl.ANY),
        ],
        out_specs=(
            pl.BlockSpec((rows, d), lambda i: (jnp.minimum(i, prompt_tiles - 1), 0)),
            pl.BlockSpec((rows, d), lambda i: (0, 0)),
        ),
        out_shape=(jax.ShapeDtypeStruct((m_p, d), F32), jax.ShapeDtypeStruct((m_s, d), F32)),
        scratch_shapes=[pltpu.VMEM((2, TOP_K, rows, d), F32), pltpu.SemaphoreType.DMA((2,))],
        compiler_params=pltpu.CompilerParams(
            dimension_semantics=("arbitrary",), vmem_limit_bytes=_vmem_limit(est)),
        name="moe_combine",
    )(pos_table, pos_table, x3_p, x3_s, gate_tok, g_final, y_sorted)


def _routing_tables(sel, gates):
    m, n_exp = sel.shape
    p_rows = _round_up(m * TOP_K + n_exp * MOE_UNIT, MOE_SUB)
    counts = jnp.sum(sel, axis=0)
    units = (counts + MOE_UNIT - 1) // MOE_UNIT
    group_rows = units * MOE_UNIT
    group_start = jnp.cumsum(group_rows) - group_rows
    rank = jnp.cumsum(sel, axis=0) - sel
    pos_full = group_start[None, :] + rank
    chosen = sel > 0
    pos_lo = jnp.min(jnp.where(chosen, pos_full, p_rows), axis=1)
    pos_hi = jnp.max(jnp.where(chosen, pos_full, -1), axis=1)
    pos_tok = jnp.stack([pos_lo, pos_hi], axis=1).astype(jnp.int32)
    gate_lo = jnp.sum(jnp.where(chosen & (pos_full == pos_lo[:, None]), gates, 0.0), axis=1)
    gate_hi = jnp.sum(jnp.where(chosen & (pos_full == pos_hi[:, None]), gates, 0.0), axis=1)
    gate_tok = jnp.stack([gate_lo, gate_hi], axis=1)
    token = jnp.broadcast_to(jnp.arange(m, dtype=jnp.int32)[:, None], (m, TOP_K))
    src = jnp.zeros((p_rows,), jnp.int32).at[pos_tok.reshape(-1)].set(
        token.reshape(-1), unique_indices=True, mode="promise_in_bounds")

    n_visits = n_exp + (p_rows // MOE_UNIT) // MOE_UNITS_PER_VISIT
    visits_per = (units + MOE_UNITS_PER_VISIT - 1) // MOE_UNITS_PER_VISIT
    visit_end = jnp.cumsum(visits_per)
    vid = jnp.arange(n_visits, dtype=jnp.int32)
    valid = vid < visit_end[-1]
    owner = jnp.minimum(vid, visit_end[-1] - 1)
    expert = jnp.sum(visit_end[None, :] <= owner[:, None], axis=1).astype(jnp.int32)
    j = vid - (visit_end - visits_per)[expert]
    visit_units = jnp.where(
        valid, jnp.clip(units[expert] - j * MOE_UNITS_PER_VISIT, 0, MOE_UNITS_PER_VISIT), 0)
    visit_row = jnp.where(valid, group_start[expert] + j * MOE_UNITS_PER_VISIT * MOE_UNIT, 0)
    total = jnp.sum(group_rows)
    tail = jnp.stack([total, (p_rows - total) // MOE_UNIT]).astype(jnp.int32)
    return (src, pos_tok, gate_tok, expert, visit_row.astype(jnp.int32),
            visit_units.astype(jnp.int32), tail)


def _gate_tables(w_s, b_s, seq_len, width):
    heads = w_s.shape[0]
    length = min(seq_len, GATE_CHUNK)
    pos = jnp.arange(length)
    mask = (pos[:, None] // CAUSAL_CHUNK) >= (pos[None, :] // CAUSAL_CHUNK)
    w = jnp.where(mask[None], w_s[:, :length, :length], 0.0)
    reps = GATE_CHUNK // length
    if reps > 1:
        w = jnp.einsum("ab,hts->hatbs", jnp.eye(reps, dtype=w.dtype), w).reshape(
            heads, GATE_CHUNK, GATE_CHUNK)
    bias = jnp.tile(b_s[:, :length].T, (reps, 1))
    bias = jnp.repeat(bias, width // heads, axis=1)
    return w.astype(BF16), bias


def kernel(x_prompt, x_sample, cache_conv_a, cache_conv_c, l0_norm_mix, l0_w_in, l0_b_in, l0_conv_w, l0_conv_b, l0_ln_a_g, l0_ln_a_b, l0_ln_v_g, l0_ln_v_b, l0_w_s, l0_b_s, l0_w_out, l0_norm_ffn, l0_ffn_gate, l0_ffn_up, l0_ffn_down, l1_norm_mix, l1_w_in, l1_conv_w, l1_w_out, l1_norm_ffn, l1_router, l1_moe_gate, l1_moe_up, l1_moe_down, final_norm):
    n_p, t_p, d = x_prompt.shape
    n_s, t_s, _ = x_sample.shape
    m_p, m_s = n_p * t_p, n_s * t_s
    a_width = l0_conv_b.shape[0]
    row = lambda vec: vec.reshape(1, -1)
    xp = x_prompt.reshape(m_p, d)
    xs = x_sample.reshape(m_s, d)

    l0_head = (row(l0_norm_mix), l0_w_in.astype(BF16), row(l0_b_in), l0_conv_w, row(l0_conv_b),
               row(l0_ln_a_g), row(l0_ln_a_b), row(l0_ln_v_g), row(l0_ln_v_b))
    wout0 = l0_w_out.astype(BF16)
    zero_a = jnp.zeros((n_p,) + cache_conv_a.shape[1:], F32)
    xp, conv_a_prompt, _ = _l0_mixer(
        xp, zero_a, l0_head + _gate_tables(l0_w_s, l0_b_s, t_p, a_width) + (wout0,),
        n_seq=n_p, seq_len=t_p, keep_v=False)
    xs, conv_a_sample, v_sample = _l0_mixer(
        xs, cache_conv_a, l0_head + _gate_tables(l0_w_s, l0_b_s, t_s, a_width) + (wout0,),
        n_seq=n_s, seq_len=t_s, keep_v=True)

    ffn_w = (row(l0_norm_ffn), l0_ffn_gate.astype(BF16), l0_ffn_up.astype(BF16),
             l0_ffn_down.astype(BF16))
    xp = _ffn(xp, *ffn_w)
    xs = _ffn(xs, *ffn_w)

    l1_w = (row(l1_norm_mix), l1_w_in.astype(BF16), l1_conv_w, l1_w_out.astype(BF16),
            row(l1_norm_ffn), l1_router)
    zero_c = jnp.zeros((n_p,) + cache_conv_c.shape[1:], F32)
    no_tail = jnp.zeros((V7X_SUBLANES, d), F32)
    xs, xn_s, gates_s, sel_s, conv_c_sample = _l1_mixer(
        xs, cache_conv_c, no_tail, l1_w, n_seq=n_s, seq_len=t_s, tail_rows=0)
    xp, xn, gates_p, sel_p, conv_c_prompt = _l1_mixer(
        xp, zero_c, xn_s, l1_w, n_seq=n_p, seq_len=t_p, tail_rows=m_s)

    sel = jnp.concatenate([sel_p, sel_s], axis=0)
    gates = jnp.concatenate([gates_p, gates_s], axis=0)
    src, pos_tok, gate_tok, v_expert, v_row, v_subs, tail = _routing_tables(sel, gates)
    x_sorted = _gather_rows(xn, src, rows=MOE_SUB, out_dtype=BF16)
    y_sorted = _moe_experts(x_sorted, v_expert, v_row, v_subs, tail,
                            l1_moe_gate, l1_moe_up, l1_moe_down)
    y_p, y_s = _combine(xp, xs, y_sorted, pos_tok, gate_tok, row(final_norm))

    return (y_p.reshape(n_p, t_p, d), y_s.reshape(n_s, t_s, d), conv_a_prompt, conv_a_sample,
            v_sample.reshape(n_s, t_s, a_width), conv_c_prompt, conv_c_sample)
```
